```python
import math
import jax, jax.numpy as jnp
from jax import lax
import numpy as np

D_MODEL = 2048
BATCH = 2
SEQ = 16384
DEPTH = 2

GRID_W = 64
CTX_LEN = 256
D_FF = 5632
EPS = 1e-6
ROPE_THETA = 10000.0
Q_BLOCK = 128
N_BRANCH = 3
N_MOD = 9

LRU_WIDTH = 1024
LRU_BLOCKS = 8
LRU_BLOCK = LRU_WIDTH // LRU_BLOCKS
CONV_W = 4
LRU_C = 8.0

MLA_HEADS = 8
MLA_Q_RANK = 512
MLA_KV_RANK = 256
MLA_NOPE = 128
MLA_ROPE = 64
MLA_V = 128
MLA_QK = MLA_NOPE + MLA_ROPE

DIFF_HEADS = 8
DIFF_D = 64
DIFF_V = 2 * DIFF_D

BRANCH_W = 1024

IN_SIZES = (LRU_WIDTH, LRU_WIDTH, MLA_Q_RANK, MLA_KV_RANK, MLA_ROPE,
            DIFF_HEADS * 2 * DIFF_D, DIFF_HEADS * 2 * DIFF_D, DIFF_HEADS * DIFF_V,
            N_BRANCH * D_MODEL)
N_IN = sum(IN_SIZES)

kernel_name = "hybrid_rglru_mla_diffattn_macaron_dit"


def rmsnorm(x, g):
    xf = x.astype(jnp.float32)
    xf = xf * lax.rsqrt(jnp.mean(xf * xf, axis=-1, keepdims=True) + EPS)
    return (xf * g.astype(jnp.float32)).astype(x.dtype)


def split_in(proj):
    offsets, acc = [], 0
    for s in IN_SIZES[:-1]:
        acc += s
        offsets.append(acc)
    return jnp.split(proj, offsets, axis=-1)


def modulate(h, mod, s):
    return h * (1.0 + mod[:, 3 * s][:, None, :]) + mod[:, 3 * s + 1][:, None, :]


def gate_of(mod, s):
    return mod[:, 3 * s + 2][:, None, :]


def swiglu(h, w_gate, w_up, w_down):
    return (jax.nn.silu(h @ w_gate) * (h @ w_up)) @ w_down


def axial_rope(n_tokens, rot_dim):
    rows = n_tokens // GRID_W
    axis_dim = rot_dim // 2
    inv_freq = ROPE_THETA ** (-jnp.arange(0, axis_dim, 2, dtype=jnp.float32) / axis_dim)
    nf = axis_dim // 2
    row_ang = jnp.arange(rows, dtype=jnp.float32)[:, None] * inv_freq
    col_ang = jnp.arange(GRID_W, dtype=jnp.float32)[:, None] * inv_freq
    ang = jnp.concatenate([jnp.broadcast_to(row_ang[:, None, :], (rows, GRID_W, nf)),
                           jnp.broadcast_to(col_ang[None, :, :], (rows, GRID_W, nf))], axis=-1)
    ang = ang.reshape(rows * GRID_W, axis_dim)
    return jnp.cos(ang), jnp.sin(ang)


def apply_rope(x, cos, sin):
    half = x.shape[-1] // 2
    shape = (1, x.shape[1]) + (1,) * (x.ndim - 3) + (half,)
    cs = cos.reshape(shape).astype(x.dtype)
    sn = sin.reshape(shape).astype(x.dtype)
    x1, x2 = x[..., :half], x[..., half:]
    return jnp.concatenate([x1 * cs - x2 * sn, x1 * sn + x2 * cs], axis=-1)


def sweep_query_blocks(fn, q):
    b, n = q.shape[:2]
    nb = n // Q_BLOCK
    qb = jnp.moveaxis(q.reshape((b, nb, Q_BLOCK) + q.shape[2:]), 1, 0)
    out = jnp.moveaxis(lax.map(fn, qb), 0, 1)
    return out.reshape((b, n) + out.shape[3:])


def centred_dwconv(x, w, b):
    n = x.shape[1]
    left = CONV_W // 2
    xp = jnp.pad(x, ((0, 0), (left, CONV_W - 1 - left), (0, 0)))
    y = b + xp[:, 0:n] * w[0]
    for j in range(1, CONV_W):
        y = y + xp[:, j:j + n] * w[j]
    return y


def block_diag(x, w, b):
    xb = x.reshape(x.shape[:-1] + (LRU_BLOCKS, LRU_BLOCK))
    return jnp.einsum('blni,nij->blnj', xb, w).reshape(x.shape) + b


def _affine_combine(left, right):
    return (left[0] * right[0], right[0] * left[1] + right[1])


def rglru_scan(x, h0, wa, ba, wx, bx, lam, reverse):
    xf = x.astype(jnp.float32)
    r = jax.nn.sigmoid(block_diag(x, wa, ba).astype(jnp.float32))
    i = jax.nn.sigmoid(block_diag(x, wx, bx).astype(jnp.float32))
    log_a = -LRU_C * r * jax.nn.softplus(-lam.astype(jnp.float32))
    a = jnp.exp(log_a)
    u = jnp.sqrt(1.0 - jnp.exp(2.0 * log_a)) * (i * xf)
    if reverse:
        a, u = a[:, ::-1], u[:, ::-1]
    u = u.at[:, 0].add(a[:, 0] * h0)
    _, h = lax.associative_scan(_affine_combine, (a, u), axis=1)
    return h[:, ::-1] if reverse else h


def rglru_branch(x_ctx, g_ctx, x_lat, g_lat, p, need_ctx_out):
    xc = centred_dwconv(x_ctx, p['lru_conv_w'], p['lru_conv_b'])
    xl = centred_dwconv(x_lat, p['lru_conv_w'], p['lru_conv_b'])
    h_zero = jnp.zeros((x_ctx.shape[0], LRU_WIDTH), jnp.float32)
    hl_sum, hc_sum = None, None
    for d, rev in enumerate((False, True)):
        prm = (p['lru_wa'][d], p['lru_ba'][d], p['lru_wx'][d], p['lru_bx'][d], p['lru_lambda'][d])
        hc = rglru_scan(xc, h_zero, *prm, reverse=rev)
        hc_final = hc[:, 0] if rev else hc[:, -1]
        hl = rglru_scan(xl, hc_final, *prm, reverse=rev)
        hl_sum = hl if hl_sum is None else hl_sum + hl
        hc_sum = hc if hc_sum is None else hc_sum + hc
    y_lat = (hl_sum * jax.nn.gelu(g_lat.astype(jnp.float32))).astype(x_lat.dtype)
    y_ctx = None
    if need_ctx_out:
        y_ctx = (hc_sum * jax.nn.gelu(g_ctx.astype(jnp.float32))).astype(x_ctx.dtype)
    return y_ctx, y_lat


def mla_qkv(qc, kvc, kr, p, rope):
    b, n = qc.shape[:2]
    q = (rmsnorm(qc, p['mla_qc_norm']) @ p['mla_w_uq']).reshape(b, n, MLA_HEADS, MLA_QK)
    kv = (rmsnorm(kvc, p['mla_kvc_norm']) @ p['mla_w_ukv']).reshape(b, n, MLA_HEADS, MLA_NOPE + MLA_V)
    k_nope, v = kv[..., :MLA_NOPE], kv[..., MLA_NOPE:]
    k_rope = jnp.broadcast_to(kr[:, :, None, :], (b, n, MLA_HEADS, MLA_ROPE))
    k = jnp.concatenate([k_nope, k_rope], axis=-1)
    q = rmsnorm(q, p['mla_q_norm'])
    k = rmsnorm(k, p['mla_k_norm'])
    if rope is not None:
        cos, sin = rope
        q = jnp.concatenate([q[..., :MLA_NOPE], apply_rope(q[..., MLA_NOPE:], cos, sin)], axis=-1)
        k = jnp.concatenate([k[..., :MLA_NOPE], apply_rope(k[..., MLA_NOPE:], cos, sin)], axis=-1)
    return q, k, v


def mla_attend(q, k, v):
    s = jnp.einsum('bqhd,bkhd->bhqk', q, k).astype(jnp.float32) * (MLA_QK ** -0.5)
    pr = jax.nn.softmax(s, axis=-1).astype(v.dtype)
    return jnp.einsum('bhqk,bkhe->bqhe', pr, v)


def diff_qkv(dq, dk, dv, p, rope):
    b, n = dq.shape[:2]
    q = rmsnorm(dq.reshape(b, n, DIFF_HEADS, 2, DIFF_D), p['diff_q_norm'])
    k = rmsnorm(dk.reshape(b, n, DIFF_HEADS, 2, DIFF_D), p['diff_k_norm'])
    v = dv.reshape(b, n, DIFF_HEADS, DIFF_V)
    if rope is not None:
        cos, sin = rope
        q = apply_rope(q, cos, sin)
        k = apply_rope(k, cos, sin)
    return q, k, v


def diff_attend(q, k, v, lam):
    s = jnp.einsum('bqhcd,bkhcd->bchqk', q, k).astype(jnp.float32) * (DIFF_D ** -0.5)
    pr = jax.nn.softmax(s, axis=-1)
    pd = (pr[:, 0] - lam * pr[:, 1]).astype(v.dtype)
    return jnp.einsum('bhqk,bkhe->bqhe', pd, v)


def diff_out(o, p, lam_init):
    b, n = o.shape[:2]
    return (rmsnorm(o, p['diff_out_norm']) * (1.0 - lam_init)).reshape(b, n, DIFF_HEADS * DIFF_V)


def merge(branches, gate_cols, branch_proj, w_out):
    g = jax.nn.sigmoid(gate_cols.reshape(gate_cols.shape[:-1] + (N_BRANCH, D_MODEL)))
    acc = g[..., 0, :] * (branches[0] @ branch_proj[0])
    for j in range(1, N_BRANCH):
        acc = acc + g[..., j, :] * (branches[j] @ branch_proj[j])
    return acc @ w_out


def mixer(h_lat, h_ctx, p, lam_init, need_ctx_out):
    b, n = h_lat.shape[:2]
    rx_l, rg_l, qc_l, kvc_l, kr_l, dq_l, dk_l, dv_l, gt_l = split_in(h_lat @ p['w_in'])
    rx_c, rg_c, qc_c, kvc_c, kr_c, dq_c, dk_c, dv_c, gt_c = split_in(h_ctx @ p['w_in'])

    a_ctx, a_lat = rglru_branch(rx_c, rg_c, rx_l, rg_l, p, need_ctx_out)

    q_ml, k_ml, v_ml = mla_qkv(qc_l, kvc_l, kr_l, p, axial_rope(n, MLA_ROPE))
    q_mc, k_mc, v_mc = mla_qkv(qc_c, kvc_c, kr_c, p, None)
    k_m_all = jnp.concatenate([k_mc, k_ml], axis=1)
    v_m_all = jnp.concatenate([v_mc, v_ml], axis=1)
    b_lat = sweep_query_blocks(lambda qi: mla_attend(qi, k_m_all, v_m_all), q_ml).reshape(b, n, MLA_HEADS * MLA_V)

    lam_p = p['diff_lambda'].astype(jnp.float32)
    lam = jnp.exp(jnp.sum(lam_p[0] * lam_p[1])) - jnp.exp(jnp.sum(lam_p[2] * lam_p[3])) + lam_init
    q_dl, k_dl, v_dl = diff_qkv(dq_l, dk_l, dv_l, p, axial_rope(n, DIFF_D))
    q_dc, k_dc, v_dc = diff_qkv(dq_c, dk_c, dv_c, p, None)
    k_d_all = jnp.concatenate([k_dc, k_dl], axis=1)
    v_d_all = jnp.concatenate([v_dc, v_dl], axis=1)
    c_lat = diff_out(sweep_query_blocks(lambda qi: diff_attend(qi, k_d_all, v_d_all, lam), q_dl), p, lam_init)

    y_lat = merge((a_lat, b_lat, c_lat), gt_l, p['branch_proj'], p['w_out'])
    y_ctx = None
    if need_ctx_out:
        bc, nc = h_ctx.shape[:2]
        b_ctx = mla_attend(q_mc, k_mc, v_mc).reshape(bc, nc, MLA_HEADS * MLA_V)
        c_ctx_o = diff_out(diff_attend(q_dc, k_dc, v_dc, lam), p, lam_init)
        y_ctx = merge((a_ctx, b_ctx, c_ctx_o), gt_c, p['branch_proj'], p['w_out'])
    return y_lat, y_ctx


def setup_inputs(seed: int = 0) -> dict:
    key = jax.random.key(seed)
    ks = jax.random.split(key, 32)

    def nrm(k, shape, fan_in, mult=1.0):
        return jax.random.normal(k, shape, jnp.float32) * (mult * fan_in ** -0.5)

    def gain(k, shape):
        return 1.0 + 0.01 * jax.random.normal(k, shape, jnp.float32)

    a0 = jax.random.uniform(ks[17], (DEPTH, 2, LRU_WIDTH), jnp.float32, minval=0.9, maxval=0.999)
    s0 = a0 ** (1.0 / LRU_C)
    lru_lambda = jnp.log(s0) - jnp.log1p(-s0)
    return {
        "x": jax.random.normal(ks[0], (BATCH, SEQ, D_MODEL), jnp.float32),
        "c": jax.random.normal(ks[1], (BATCH, D_MODEL), jnp.float32),
        "ctx": jax.random.normal(ks[2], (BATCH, CTX_LEN, D_MODEL), jnp.float32),
        "c_ctx": jax.random.normal(ks[3], (D_MODEL,), jnp.float32),
        "w_mod": nrm(ks[4], (DEPTH, D_MODEL, N_MOD * D_MODEL), D_MODEL, 0.5),
        "b_mod": 0.01 * jax.random.normal(ks[5], (DEPTH, N_MOD * D_MODEL), jnp.float32),
        "norm_g": gain(ks[6], (DEPTH, 3, D_MODEL)),
        "ffn_w_gate": nrm(ks[7], (DEPTH, 2, D_MODEL, D_FF), D_MODEL),
        "ffn_w_up": nrm(ks[8], (DEPTH, 2, D_MODEL, D_FF), D_MODEL),
        "ffn_w_down": nrm(ks[9], (DEPTH, 2, D_FF, D_MODEL), D_FF),
        "w_in": nrm(ks[10], (DEPTH, D_MODEL, N_IN), D_MODEL),
        "lru_conv_w": nrm(ks[11], (DEPTH, CONV_W, LRU_WIDTH), CONV_W),
        "lru_conv_b": 0.01 * jax.random.normal(ks[12], (DEPTH, LRU_WIDTH), jnp.float32),
        "lru_wa": nrm(ks[13], (DEPTH, 2, LRU_BLOCKS, LRU_BLOCK, LRU_BLOCK), LRU_BLOCK),
        "lru_ba": 0.01 * jax.random.normal(ks[14], (DEPTH, 2, LRU_WIDTH), jnp.float32),
        "lru_wx": nrm(ks[15], (DEPTH, 2, LRU_BLOCKS, LRU_BLOCK, LRU_BLOCK), LRU_BLOCK),
        "lru_bx": 0.01 * jax.random.normal(ks[16], (DEPTH, 2, LRU_WIDTH), jnp.float32),
        "lru_lambda": lru_lambda,
        "mla_qc_norm": gain(ks[18], (DEPTH, MLA_Q_RANK)),
        "mla_kvc_norm": gain(ks[19], (DEPTH, MLA_KV_RANK)),
        "mla_w_uq": nrm(ks[20], (DEPTH, MLA_Q_RANK, MLA_HEADS * MLA_QK), MLA_Q_RANK),
        "mla_w_ukv": nrm(ks[21], (DEPTH, MLA_KV_RANK, MLA_HEADS * (MLA_NOPE + MLA_V)), MLA_KV_RANK),
        "mla_q_norm": gain(ks[22], (DEPTH, MLA_QK)),
        "mla_k_norm": gain(ks[23], (DEPTH, MLA_QK)),
        "diff_q_norm": gain(ks[24], (DEPTH, DIFF_D)),
        "diff_k_norm": gain(ks[25], (DEPTH, DIFF_D)),
        "diff_lambda": 0.1 * jax.random.normal(ks[26], (DEPTH, 4, DIFF_D), jnp.float32),
        "diff_out_norm": gain(ks[27], (DEPTH, DIFF_V)),
        "branch_proj": nrm(ks[28], (DEPTH, N_BRANCH, BRANCH_W, D_MODEL), BRANCH_W),
        "w_out": nrm(ks[29], (DEPTH, D_MODEL, D_MODEL), D_MODEL),
    }


def reference(x, c, ctx, c_ctx, w_mod, b_mod, norm_g, ffn_w_gate, ffn_w_up, ffn_w_down,
              w_in, lru_conv_w, lru_conv_b, lru_wa, lru_ba, lru_wx, lru_bx, lru_lambda,
              mla_qc_norm, mla_kvc_norm, mla_w_uq, mla_w_ukv, mla_q_norm, mla_k_norm,
              diff_q_norm, diff_k_norm, diff_lambda, diff_out_norm, branch_proj, w_out):
    xl, xc = x, ctx
    silu_c = jax.nn.silu(c)
    silu_cc = jax.nn.silu(c_ctx)[None, :]
    for l in range(DEPTH):
        last = l == DEPTH - 1
        p = {
            'w_in': w_in[l], 'lru_conv_w': lru_conv_w[l], 'lru_conv_b': lru_conv_b[l],
            'lru_wa': lru_wa[l], 'lru_ba': lru_ba[l], 'lru_wx': lru_wx[l], 'lru_bx': lru_bx[l],
            'lru_lambda': lru_lambda[l], 'mla_qc_norm': mla_qc_norm[l], 'mla_kvc_norm': mla_kvc_norm[l],
            'mla_w_uq': mla_w_uq[l], 'mla_w_ukv': mla_w_ukv[l], 'mla_q_norm': mla_q_norm[l],
            'mla_k_norm': mla_k_norm[l], 'diff_q_norm': diff_q_norm[l], 'diff_k_norm': diff_k_norm[l],
            'diff_lambda': diff_lambda[l], 'diff_out_norm': diff_out_norm[l],
            'branch_proj': branch_proj[l], 'w_out': w_out[l],
        }
        mod_l = (silu_c @ w_mod[l] + b_mod[l]).reshape(-1, N_MOD, D_MODEL)
        mod_c = (silu_cc @ w_mod[l] + b_mod[l]).reshape(1, N_MOD, D_MODEL)
        lam_init = 0.8 - 0.6 * math.exp(-0.3 * l)

        f1 = (ffn_w_gate[l, 0], ffn_w_up[l, 0], ffn_w_down[l, 0])
        xl = xl + 0.5 * gate_of(mod_l, 0) * swiglu(modulate(rmsnorm(xl, norm_g[l, 0]), mod_l, 0), *f1)
        xc = xc + 0.5 * gate_of(mod_c, 0) * swiglu(modulate(rmsnorm(xc, norm_g[l, 0]), mod_c, 0), *f1)

        h_lat = modulate(rmsnorm(xl, norm_g[l, 1]), mod_l, 1)
        h_ctx = modulate(rmsnorm(xc, norm_g[l, 1]), mod_c, 1)
        y_lat, y_ctx = mixer(h_lat, h_ctx, p, lam_init, not last)
        xl = xl + gate_of(mod_l, 1) * y_lat

        f2 = (ffn_w_gate[l, 1], ffn_w_up[l, 1], ffn_w_down[l, 1])
        xl = xl + 0.5 * gate_of(mod_l, 2) * swiglu(modulate(rmsnorm(xl, norm_g[l, 2]), mod_l, 2), *f2)
        if not last:
            xc = xc + gate_of(mod_c, 1) * y_ctx
            xc = xc + 0.5 * gate_of(mod_c, 2) * swiglu(modulate(rmsnorm(xc, norm_g[l, 2]), mod_c, 2), *f2)
    return xl
```

```python
import functools
import math

import jax
import jax.numpy as jnp
from jax import lax
from jax.experimental import pallas as pl
from jax.experimental.pallas import tpu as pltpu

F32 = jnp.float32
BF16 = jnp.bfloat16

EPS = 1e-6
ROPE_THETA = 10000.0
GRID_W = 64
N_BRANCH = 3
N_MOD = 9
LRU_BLOCKS = 8
CONV_W = 4
LRU_C = 8.0
MLA_HEADS = 8
MLA_NOPE = 128
MLA_ROPE = 64
MLA_V = 128
MLA_QK = MLA_NOPE + MLA_ROPE
DIFF_HEADS = 8
DIFF_D = 64
DIFF_V = 2 * DIFF_D

LANE = 128
SUBLANE = 8
MLA_QK_PAD = 256
VMEM_LIMIT = 56 * 1024 * 1024


def _cparams(*sem):
    return pltpu.CompilerParams(dimension_semantics=sem, vmem_limit_bytes=VMEM_LIMIT)


def _tile(n, pref):
    if n <= pref:
        return n
    t = pref - pref % SUBLANE
    while t > SUBLANE and n % t:
        t -= SUBLANE
    assert n % t == 0, (n, pref)
    return t


def _norm_mod(x, g, m0, m1):
    ms = jnp.mean(x * x, axis=-1, keepdims=True)
    hn = (x * lax.rsqrt(ms + EPS)) * g
    return hn * (1.0 + m0) + m1


def _mod_kernel(c_ref, w_ref, b_ref, o_ref):
    c = c_ref[...]
    sc = c * jax.nn.sigmoid(c)
    o_ref[0] = jnp.dot(sc, w_ref[0], preferred_element_type=F32) + b_ref[0]


def _compute_mod(cpad, w_mod, b_mod):
    L, D, ND = w_mod.shape
    R = cpad.shape[0]
    tn = _tile(ND, 1024)
    return pl.pallas_call(
        _mod_kernel,
        grid=(L, ND // tn),
        in_specs=[pl.BlockSpec((R, D), lambda l, j: (0, 0)),
                  pl.BlockSpec((1, D, tn), lambda l, j: (l, 0, j)),
                  pl.BlockSpec((1, 1, tn), lambda l, j: (l, 0, j))],
        out_specs=pl.BlockSpec((1, R, tn), lambda l, j: (l, 0, j)),
        out_shape=jax.ShapeDtypeStruct((L, R, ND), F32),
        compiler_params=_cparams("parallel", "parallel"),
        name="adaln_mod",
    )(cpad, w_mod, b_mod.reshape(L, 1, ND))


def _ffn_kernel(x_ref, g_ref, m0_ref, m1_ref, m2_ref, wg_ref, wu_ref, wd_ref, o_ref, h_scr):
    f = pl.program_id(2)

    @pl.when(f == 0)
    def _():
        h_scr[...] = _norm_mod(x_ref[0], g_ref[...], m0_ref[0], m1_ref[0]).astype(BF16)
        o_ref[0] = jnp.zeros(o_ref.shape[1:], F32)

    h = h_scr[...]
    g = jnp.dot(h, wg_ref[...], preferred_element_type=F32)
    u = jnp.dot(h, wu_ref[...], preferred_element_type=F32)
    a = (g * jax.nn.sigmoid(g)) * u
    o_ref[0] += jnp.dot(a.astype(BF16), wd_ref[...], preferred_element_type=F32)

    @pl.when(f == pl.num_programs(2) - 1)
    def _():
        o_ref[0] = x_ref[0] + (0.5 * m2_ref[0]) * o_ref[0]


def _ffn(x, g, m0, m1, m2, wg, wu, wd):
    B, N, D = x.shape
    F = wg.shape[1]
    tm = _tile(N, 512)
    fk = _tile(F, 512)
    tok = pl.BlockSpec((1, tm, D), lambda b, i, f: (b, i, 0))
    row = pl.BlockSpec((1, 1, D), lambda b, i, f: (b, 0, 0))
    return pl.pallas_call(
        _ffn_kernel,
        grid=(B, N // tm, F // fk),
        in_specs=[tok, pl.BlockSpec((1, D), lambda b, i, f: (0, 0)), row, row, row,
                  pl.BlockSpec((D, fk), lambda b, i, f: (0, f)),
                  pl.BlockSpec((D, fk), lambda b, i, f: (0, f)),
                  pl.BlockSpec((fk, D), lambda b, i, f: (f, 0))],
        out_specs=tok,
        out_shape=jax.ShapeDtypeStruct((B, N, D), F32),
        scratch_shapes=[pltpu.VMEM((tm, D), BF16)],
        compiler_params=_cparams("parallel", "parallel", "arbitrary"),
        name="swiglu_half_step",
    )(x, g, m0, m1, m2, wg, wu, wd)


def _inproj_kernel(x_ref, g_ref, m0_ref, m1_ref, w_ref, o_ref, h_scr):
    @pl.when(pl.program_id(2) == 0)
    def _():
        h_scr[...] = _norm_mod(x_ref[0], g_ref[...], m0_ref[0], m1_ref[0]).astype(BF16)

    o_ref[0] = jnp.dot(h_scr[...], w_ref[...], preferred_element_type=F32)


def _inproj(x, g, m0, m1, w):
    B, N, D = x.shape
    NP = w.shape[1]
    tm = _tile(N, 512)
    tn = _tile(NP, 2048)
    row = pl.BlockSpec((1, 1, D), lambda b, i, j: (b, 0, 0))
    return pl.pallas_call(
        _inproj_kernel,
        grid=(B, N // tm, NP // tn),
        in_specs=[pl.BlockSpec((1, tm, D), lambda b, i, j: (b, i, 0)),
                  pl.BlockSpec((1, D), lambda b, i, j: (0, 0)), row, row,
                  pl.BlockSpec((D, tn), lambda b, i, j: (0, j))],
        out_specs=pl.BlockSpec((1, tm, tn), lambda b, i, j: (b, i, j)),
        out_shape=jax.ShapeDtypeStruct((B, N, NP), F32),
        scratch_shapes=[pltpu.VMEM((tm, D), BF16)],
        compiler_params=_cparams("parallel", "parallel", "arbitrary"),
        name="mixer_in_proj",
    )(x, g, m0, m1, w)


def _lru_kernel(x_ref, xp_ref, xn_ref, cw_ref, cb_ref, wa_ref, ba_ref, wx_ref, bx_ref, lam_ref, h0_ref,
                h_ref, hfin_ref, xs_scr, a_scr, u_scr, carry_scr, *, reverse, nt, tm, W):
    i = pl.program_id(1)
    ti = (nt - 1 - i) if reverse else i
    H = SUBLANE

    @pl.when(i == 0)
    def _():
        carry_scr[...] = h0_ref[0]

    zero = jnp.zeros((H, W), F32)
    xs_scr[0:H, :] = jnp.where(ti > 0, xp_ref[0], zero)
    xs_scr[H:H + tm, :] = x_ref[0]
    xs_scr[H + tm:2 * H + tm, :] = jnp.where(ti < nt - 1, xn_ref[0], zero)

    sp = jax.nn.softplus(-lam_ref[...])
    lb = W // LRU_BLOCKS
    left = CONV_W // 2
    for n in range(LRU_BLOCKS):
        sl = slice(n * lb, (n + 1) * lb)
        xc = cb_ref[:, sl] + xs_scr[H - left:H - left + tm, sl] * cw_ref[0:1, sl]
        for j in range(1, CONV_W):
            xc = xc + xs_scr[H - left + j:H - left + j + tm, sl] * cw_ref[j:j + 1, sl]
        xb = xc.astype(BF16)
        r = jax.nn.sigmoid(jnp.dot(xb, wa_ref[n], preferred_element_type=F32) + ba_ref[:, sl])
        ig = jax.nn.sigmoid(jnp.dot(xb, wx_ref[n], preferred_element_type=F32) + bx_ref[:, sl])
        log_a = (-LRU_C) * r * sp[:, sl]
        a_scr[:, sl] = jnp.exp(log_a)
        u_scr[:, sl] = jnp.sqrt(1.0 - jnp.exp(2.0 * log_a)) * (ig * xc)

    rowi = lax.broadcasted_iota(jnp.int32, (H, W), 0)
    ng = tm // H

    def group(gi, h):
        g = (ng - 1 - gi) if reverse else gi
        r0 = pl.multiple_of(g * H, H)
        a = a_scr[pl.ds(r0, H), :]
        u = u_scr[pl.ds(r0, H), :]
        for s in (1, 2, 4):
            sh = (H - s) if reverse else s
            valid = (rowi < H - s) if reverse else (rowi >= s)
            a_s = pltpu.roll(a, sh, 0)
            u_s = pltpu.roll(u, sh, 0)
            u = jnp.where(valid, a * u_s + u, u)
            a = jnp.where(valid, a * a_s, a)
        hh = a * h + u
        h_ref[0, pl.ds(r0, H), :] = hh
        return hh[0:1, :] if reverse else hh[H - 1:H, :]

    h_last = lax.fori_loop(0, ng, group, carry_scr[...])
    carry_scr[...] = h_last
    hfin_ref[0] = h_last


def _lru_scan(proj, col_blk, h0, cw, cb, wa, ba, wx, bx, lam, *, reverse, W):
    B, N, _ = proj.shape
    tm = _tile(N, 256)
    nt = N // tm
    r8 = tm // SUBLANE
    nb8 = N // SUBLANE

    def t_of(i):
        return (nt - 1 - i) if reverse else i

    vec = pl.BlockSpec((1, W), lambda b, i: (0, 0))
    blk = pl.BlockSpec((LRU_BLOCKS, W // LRU_BLOCKS, W // LRU_BLOCKS), lambda b, i: (0, 0, 0))
    kern = functools.partial(_lru_kernel, reverse=reverse, nt=nt, tm=tm, W=W)
    return pl.pallas_call(
        kern,
        grid=(B, nt),
        in_specs=[pl.BlockSpec((1, tm, W), lambda b, i: (b, t_of(i), col_blk)),
                  pl.BlockSpec((1, SUBLANE, W), lambda b, i: (b, jnp.maximum(t_of(i) * r8 - 1, 0), col_blk)),
                  pl.BlockSpec((1, SUBLANE, W), lambda b, i: (b, jnp.minimum((t_of(i) + 1) * r8, nb8 - 1), col_blk)),
                  pl.BlockSpec((CONV_W, W), lambda b, i: (0, 0)), vec, blk, vec, blk, vec, vec,
                  pl.BlockSpec((1, 1, W), lambda b, i: (b, 0, 0))],
        out_specs=[pl.BlockSpec((1, tm, W), lambda b, i: (b, t_of(i), 0)),
                   pl.BlockSpec((1, 1, W), lambda b, i: (b, 0, 0))],
        out_shape=[jax.ShapeDtypeStruct((B, N, W), F32), jax.ShapeDtypeStruct((B, 1, W), F32)],
        scratch_shapes=[pltpu.VMEM((tm + 2 * SUBLANE, W), F32), pltpu.VMEM((tm, W), F32),
                        pltpu.VMEM((tm, W), F32), pltpu.VMEM((1, W), F32)],
        compiler_params=_cparams("parallel", "arbitrary"),
        name="rglru_rev" if reverse else "rglru_fwd",
    )(proj, proj, proj, cw, cb, wa, ba, wx, bx, lam, h0)


def _lru_out_kernel(hf_ref, hb_ref, g_ref, o_ref):
    o_ref[0] = ((hf_ref[0] + hb_ref[0]) * jax.nn.gelu(g_ref[0])).astype(BF16)


def _lru_out(hf, hb, proj, col_blk):
    B, N, W = hf.shape
    tm = _tile(N, 512)
    tok = pl.BlockSpec((1, tm, W), lambda b, i: (b, i, 0))
    return pl.pallas_call(
        _lru_out_kernel,
        grid=(B, N // tm),
        in_specs=[tok, tok, pl.BlockSpec((1, tm, W), lambda b, i: (b, i, col_blk))],
        out_specs=tok,
        out_shape=jax.ShapeDtypeStruct((B, N, W), BF16),
        compiler_params=_cparams("parallel", "parallel"),
        name="rglru_gate_out",
    )(hf, hb, proj)


def _rope(x, cosf, sa, sb):
    return x * cosf + pltpu.roll(x, LANE - MLA_ROPE // 2, 1) * sa + pltpu.roll(x, MLA_ROPE // 2, 1) * sb


def _mla_prep_kernel(qc_ref, kvc_ref, kr_ref, gqc_ref, gkvc_ref, wuq_ref, wukv_ref, gq_ref, gk_ref,
                     cos_ref, sa_ref, sb_ref, q_ref, k_ref, v_ref):
    cosf, sa, sb = cos_ref[...], sa_ref[...], sb_ref[...]
    scale = MLA_QK ** -0.5

    qc = qc_ref[0]
    qn = (qc * lax.rsqrt(jnp.mean(qc * qc, axis=-1, keepdims=True) + EPS)) * gqc_ref[...]
    qf = jnp.dot(qn.astype(BF16), wuq_ref[...], preferred_element_type=F32)
    kvc = kvc_ref[0]
    kvn = (kvc * lax.rsqrt(jnp.mean(kvc * kvc, axis=-1, keepdims=True) + EPS)) * gkvc_ref[...]
    kvf = jnp.dot(kvn.astype(BF16), wukv_ref[...], preferred_element_type=F32)

    kr = kr_ref[0]
    ss_kr = jnp.sum(kr * kr, axis=-1, keepdims=True)
    kr_rot = _rope(kr * gk_ref[:, LANE:], cosf, sa, sb)

    for h in range(MLA_HEADS):
        o = h * MLA_QK_PAD
        qa = qf[:, o:o + LANE]
        qb = qf[:, o + LANE:o + 2 * LANE]
        ss = jnp.sum(qa * qa, axis=-1, keepdims=True) + jnp.sum(qb * qb, axis=-1, keepdims=True)
        rinv = lax.rsqrt(ss * (1.0 / MLA_QK) + EPS)
        q_ref[0, h, :, 0:LANE] = ((qa * rinv) * gq_ref[:, 0:LANE] * scale).astype(BF16)
        q_ref[0, h, :, LANE:] = (_rope((qb * rinv) * gq_ref[:, LANE:], cosf, sa, sb) * scale).astype(BF16)

        kn = kvf[:, o:o + LANE]
        ssk = jnp.sum(kn * kn, axis=-1, keepdims=True) + ss_kr
        rk = lax.rsqrt(ssk * (1.0 / MLA_QK) + EPS)
        k_ref[0, h, :, 0:LANE] = ((kn * rk) * gk_ref[:, 0:LANE]).astype(BF16)
        k_ref[0, h, :, LANE:] = (kr_rot * rk).astype(BF16)
        v_ref[0, h] = kvf[:, o + LANE:o + 2 * LANE].astype(BF16)


def _mla_prep(proj, cols, gqc, gkvc, wuq, wukv, gq, gk, cosf, sa, sb):
    B, N, _ = proj.shape
    tm = _tile(N, 256)
    QR, KR = gqc.shape[1], gkvc.shape[1]
    HP = MLA_HEADS * MLA_QK_PAD
    tab = pl.BlockSpec((tm, LANE), lambda b, i: (i, 0))

    def full(a):
        return pl.BlockSpec(a.shape, lambda b, i: (0,) * a.ndim)

    hd = lambda w: pl.BlockSpec((1, MLA_HEADS, tm, w), lambda b, i: (b, 0, i, 0))
    return pl.pallas_call(
        _mla_prep_kernel,
        grid=(B, N // tm),
        in_specs=[pl.BlockSpec((1, tm, QR), lambda b, i: (b, i, cols["qc"] // QR)),
                  pl.BlockSpec((1, tm, KR), lambda b, i: (b, i, cols["kvc"] // KR)),
                  pl.BlockSpec((1, tm, LANE), lambda b, i: (b, i, cols["kr"] // LANE)),
                  full(gqc), full(gkvc), full(wuq), full(wukv), full(gq), full(gk), tab, tab, tab],
        out_specs=[hd(MLA_QK_PAD), hd(MLA_QK_PAD), hd(MLA_V)],
        out_shape=[jax.ShapeDtypeStruct((B, MLA_HEADS, N, MLA_QK_PAD), BF16),
                   jax.ShapeDtypeStruct((B, MLA_HEADS, N, MLA_QK_PAD), BF16),
                   jax.ShapeDtypeStruct((B, MLA_HEADS, N, MLA_V), BF16)],
        compiler_params=_cparams("parallel", "parallel"),
        name="mla_qkv_prep",
    )(proj, proj, proj, gqc, gkvc, wuq, wukv, gq, gk, cosf, sa, sb)


def _diff_prep_kernel(dq_ref, dk_ref, dv_ref, gq_ref, gk_ref, cos_ref, sa_ref, sb_ref, q_ref, k_ref, v_ref):
    cosf, sa, sb = cos_ref[...], sa_ref[...], sb_ref[...]
    tm = dq_ref.shape[1]
    first = lax.broadcasted_iota(jnp.int32, (tm, LANE), 1) < DIFF_D
    scale = DIFF_D ** -0.5

    def norm_rope(x, g):
        xx = x * x
        s0 = jnp.sum(jnp.where(first, xx, 0.0), axis=-1, keepdims=True)
        s1 = jnp.sum(jnp.where(first, 0.0, xx), axis=-1, keepdims=True)
        rinv = jnp.where(first, lax.rsqrt(s0 * (1.0 / DIFF_D) + EPS), lax.rsqrt(s1 * (1.0 / DIFF_D) + EPS))
        return _rope((x * rinv) * g, cosf, sa, sb)

    for h in range(DIFF_HEADS):
        sl = slice(h * LANE, (h + 1) * LANE)
        q_ref[0, h] = (norm_rope(dq_ref[0, :, sl], gq_ref[...]) * scale).astype(BF16)
        k_ref[0, h] = norm_rope(dk_ref[0, :, sl], gk_ref[...]).astype(BF16)
        v_ref[0, h] = dv_ref[0, :, sl].astype(BF16)


def _diff_prep(proj, cols, gq2, gk2, cosf, sa, sb):
    B, N, _ = proj.shape
    tm = _tile(N, 256)
    HW = DIFF_HEADS * LANE
    tab = pl.BlockSpec((tm, LANE), lambda b, i: (i, 0))
    g = pl.BlockSpec((1, LANE), lambda b, i: (0, 0))
    hd = pl.BlockSpec((1, DIFF_HEADS, tm, LANE), lambda b, i: (b, 0, i, 0))
    shp = jax.ShapeDtypeStruct((B, DIFF_HEADS, N, LANE), BF16)
    return pl.pallas_call(
        _diff_prep_kernel,
        grid=(B, N // tm),
        in_specs=[pl.BlockSpec((1, tm, HW), lambda b, i: (b, i, cols["dq"] // HW)),
                  pl.BlockSpec((1, tm, HW), lambda b, i: (b, i, cols["dk"] // HW)),
                  pl.BlockSpec((1, tm, HW), lambda b, i: (b, i, cols["dv"] // HW)),
                  g, g, tab, tab, tab],
        out_specs=[hd, hd, hd],
        out_shape=[shp, shp, shp],
        compiler_params=_cparams("parallel", "parallel"),
        name="diff_qkv_prep",
    )(proj, proj, proj, gq2, gk2, cosf, sa, sb)


def _online_softmax_step(s, v, m_scr, l_scr, acc_scr):
    m_prev = m_scr[...]
    m_new = jnp.maximum(m_prev, jnp.max(s, axis=1, keepdims=True))
    alpha = jnp.exp(m_prev - m_new)
    tk = s.shape[1]
    m_b = jnp.concatenate([m_new] * (tk // LANE), axis=1) if tk % LANE == 0 else m_new[:, 0:1]
    p = jnp.exp(s - m_b)
    l_scr[...] = alpha * l_scr[...] + jnp.sum(p, axis=1, keepdims=True)
    acc_scr[...] = alpha * acc_scr[...] + jnp.dot(p.astype(BF16), v, preferred_element_type=F32)
    m_scr[...] = m_new


def _qk(q, k):
    return lax.dot_general(q, k, (((1,), (1,)), ((), ())), preferred_element_type=F32)


def _mla_attn_kernel(*refs, n_kv, tk):
    q_ref = refs[0]
    kv_refs = refs[1:1 + 2 * n_kv]
    o_ref = refs[1 + 2 * n_kv]
    m_scr, l_scr, acc_scr = refs[2 + 2 * n_kv:]
    m_scr[...] = jnp.full(m_scr.shape, -jnp.inf, F32)
    l_scr[...] = jnp.zeros(l_scr.shape, F32)
    acc_scr[...] = jnp.zeros(acc_scr.shape, F32)
    q = q_ref[0, 0]

    for c in range(n_kv):
        k_ref, v_ref = kv_refs[2 * c], kv_refs[2 * c + 1]
        nk = k_ref.shape[2]
        t = _tile(nk, tk)

        def body(j, carry, k_ref=k_ref, v_ref=v_ref, t=t):
            r0 = pl.multiple_of(j * t, t)
            _online_softmax_step(_qk(q, k_ref[0, 0, pl.ds(r0, t), :]), v_ref[0, 0, pl.ds(r0, t), :],
                                 m_scr, l_scr, acc_scr)
            return carry

        lax.fori_loop(0, nk // t, body, 0)

    o_ref[0] = (acc_scr[...] / l_scr[...]).astype(BF16)


def _mla_attn(q, kvs):
    B, H, N, DQ = q.shape
    tq = _tile(N, 1024)
    in_specs = [pl.BlockSpec((1, 1, tq, DQ), lambda b, h, i: (b, h, i, 0))]
    args = [q]
    for k, v in kvs:
        nk = k.shape[2]
        in_specs += [pl.BlockSpec((1, 1, nk, DQ), lambda b, h, i: (b, h, 0, 0)),
                     pl.BlockSpec((1, 1, nk, MLA_V), lambda b, h, i: (b, h, 0, 0))]
        args += [k, v]
    stat = pltpu.VMEM((tq, LANE), F32)
    return pl.pallas_call(
        functools.partial(_mla_attn_kernel, n_kv=len(kvs), tk=512),
        grid=(B, H, N // tq),
        in_specs=in_specs,
        out_specs=pl.BlockSpec((1, tq, MLA_V), lambda b, h, i: (b, i, h)),
        out_shape=jax.ShapeDtypeStruct((B, N, H * MLA_V), BF16),
        scratch_shapes=[stat, stat, stat],
        compiler_params=_cparams("parallel", "parallel", "arbitrary"),
        name="mla_attention",
    )(*args)


def _diff_attn_kernel(*refs, n_kv, tk, lam_init):
    q_ref = refs[0]
    kv_refs = refs[1:1 + 2 * n_kv]
    lamp_ref, gout_ref, o_ref = refs[1 + 2 * n_kv:4 + 2 * n_kv]
    m0, l0, acc0, m1, l1, acc1 = refs[4 + 2 * n_kv:]
    for m in (m0, m1):
        m[...] = jnp.full(m.shape, -jnp.inf, F32)
    for z in (l0, l1, acc0, acc1):
        z[...] = jnp.zeros(z.shape, F32)
    q = q_ref[0, 0]
    first = lax.broadcasted_iota(jnp.int32, q.shape, 1) < DIFF_D
    zero = jnp.zeros(q.shape, BF16)
    qa = jnp.where(first, q, zero)
    qb = jnp.where(first, zero, q)

    for c in range(n_kv):
        k_ref, v_ref = kv_refs[2 * c], kv_refs[2 * c + 1]
        nk = k_ref.shape[2]
        t = _tile(nk, tk)

        def body(j, carry, k_ref=k_ref, v_ref=v_ref, t=t):
            r0 = pl.multiple_of(j * t, t)
            k = k_ref[0, 0, pl.ds(r0, t), :]
            v = v_ref[0, 0, pl.ds(r0, t), :]
            _online_softmax_step(_qk(qa, k), v, m0, l0, acc0)
            _online_softmax_step(_qk(qb, k), v, m1, l1, acc1)
            return carry

        lax.fori_loop(0, nk // t, body, 0)

    lp = lamp_ref[...]
    lam = (jnp.exp(jnp.sum(lp[0:1] * lp[1:2], keepdims=True))
           - jnp.exp(jnp.sum(lp[2:3] * lp[3:4], keepdims=True)) + lam_init)
    o = acc0[...] / l0[...] - lam * (acc1[...] / l1[...])
    on = (o * lax.rsqrt(jnp.mean(o * o, axis=-1, keepdims=True) + EPS)) * gout_ref[...]
    o_ref[0] = (on * (1.0 - lam_init)).astype(BF16)


def _diff_attn(q, kvs, lam_p, g_out, lam_init):
    B, H, N, DQ = q.shape
    tq = _tile(N, 1024)
    in_specs = [pl.BlockSpec((1, 1, tq, DQ), lambda b, h, i: (b, h, i, 0))]
    args = [q]
    for k, v in kvs:
        nk = k.shape[2]
        kv_spec = pl.BlockSpec((1, 1, nk, LANE), lambda b, h, i: (b, h, 0, 0))
        in_specs += [kv_spec, kv_spec]
        args += [k, v]
    in_specs += [pl.BlockSpec(lam_p.shape, lambda b, h, i: (0, 0)),
                 pl.BlockSpec((1, DIFF_V), lambda b, h, i: (0, 0))]
    args += [lam_p, g_out]
    stat = pltpu.VMEM((tq, LANE), F32)
    return pl.pallas_call(
        functools.partial(_diff_attn_kernel, n_kv=len(kvs), tk=512, lam_init=lam_init),
        grid=(B, H, N // tq),
        in_specs=in_specs,
        out_specs=pl.BlockSpec((1, tq, DIFF_V), lambda b, h, i: (b, i, h)),
        out_shape=jax.ShapeDtypeStruct((B, N, H * DIFF_V), BF16),
        scratch_shapes=[stat] * 6,
        compiler_params=_cparams("parallel", "parallel", "arbitrary"),
        name="diff_attention",
    )(*args)


def _merge_kernel(a_ref, b_ref, c_ref, gt_ref, p_ref, o_ref, acc_scr):
    j = pl.program_id(2)

    def contrib(br_ref):
        return jax.nn.sigmoid(gt_ref[0]) * jnp.dot(br_ref[0], p_ref[0], preferred_element_type=F32)

    @pl.when(j == 0)
    def _():
        acc_scr[...] = contrib(a_ref)

    @pl.when(j == 1)
    def _():
        acc_scr[...] += contrib(b_ref)

    @pl.when(j == 2)
    def _():
        o_ref[0] = (acc_scr[...] + contrib(c_ref)).astype(BF16)


def _merge(a, b, c, proj, bproj):
    B, N, W = a.shape
    D = bproj.shape[2]
    tm = _tile(N, 512)
    br = pl.BlockSpec((1, tm, W), lambda b_, i, j: (b_, i, 0))
    return pl.pallas_call(
        _merge_kernel,
        grid=(B, N // tm, N_BRANCH),
        in_specs=[br, br, br,
                  pl.BlockSpec((1, tm, D), lambda b_, i, j: (b_, i, j)),
                  pl.BlockSpec((1, W, D), lambda b_, i, j: (j, 0, 0))],
        out_specs=pl.BlockSpec((1, tm, D), lambda b_, i, j: (b_, i, 0)),
        out_shape=jax.ShapeDtypeStruct((B, N, D), BF16),
        scratch_shapes=[pltpu.VMEM((tm, D), F32)],
        compiler_params=_cparams("parallel", "parallel", "arbitrary"),
        name="branch_merge",
    )(a, b, c, proj, bproj)


def _outproj_kernel(x_ref, y_ref, w_ref, m2_ref, o_ref):
    o_ref[0] = x_ref[0] + m2_ref[0] * jnp.dot(y_ref[0], w_ref[...], preferred_element_type=F32)


def _outproj(x, y, w, m2):
    B, N, D = x.shape
    tm = _tile(N, 512)
    tok = pl.BlockSpec((1, tm, D), lambda b, i: (b, i, 0))
    return pl.pallas_call(
        _outproj_kernel,
        grid=(B, N // tm),
        in_specs=[tok, tok, pl.BlockSpec((D, D), lambda b, i: (0, 0)),
                  pl.BlockSpec((1, 1, D), lambda b, i: (b, 0, 0))],
        out_specs=tok,
        out_shape=jax.ShapeDtypeStruct((B, N, D), F32),
        compiler_params=_cparams("parallel", "parallel"),
        name="mixer_out_proj",
    )(x, y, w, m2)


def _rope_tables(n_tokens, identity):
    half = MLA_ROPE // 2
    if identity:
        return (jnp.ones((n_tokens, LANE), F32), jnp.zeros((n_tokens, LANE), F32),
                jnp.zeros((n_tokens, LANE), F32))
    rows = n_tokens // GRID_W
    axis_dim = MLA_ROPE // 2
    inv_freq = ROPE_THETA ** (-jnp.arange(0, axis_dim, 2, dtype=F32) / axis_dim)
    nf = axis_dim // 2
    row_ang = jnp.arange(rows, dtype=F32)[:, None] * inv_freq
    col_ang = jnp.arange(GRID_W, dtype=F32)[:, None] * inv_freq
    ang = jnp.concatenate([jnp.broadcast_to(row_ang[:, None, :], (rows, GRID_W, nf)),
                           jnp.broadcast_to(col_ang[None, :, :], (rows, GRID_W, nf))], axis=-1)
    ang = ang.reshape(rows * GRID_W, axis_dim)
    cos, sin = jnp.cos(ang), jnp.sin(ang)
    z = jnp.zeros_like(sin)
    assert half == axis_dim
    return (jnp.concatenate([cos, cos, cos, cos], axis=1),
            jnp.concatenate([-sin, z, -sin, z], axis=1),
            jnp.concatenate([z, sin, z, sin], axis=1))


def _in_proj_layout(D, W):
    order = [("gt", N_BRANCH * D), ("rx", W), ("rg", W), ("dq", DIFF_HEADS * 2 * DIFF_D),
             ("dk", DIFF_HEADS * 2 * DIFF_D), ("dv", DIFF_HEADS * DIFF_V)]
    cols, off = {}, 0
    for name, width in order:
        if name != "gt":
            off = -(-off // width) * width
        cols[name] = off
        off += width
    return cols, off


def kernel(x, c, ctx, c_ctx, w_mod, b_mod, norm_g, ffn_w_gate, ffn_w_up, ffn_w_down, w_in, lru_conv_w,
           lru_conv_b, lru_wa, lru_ba, lru_wx, lru_bx, lru_lambda, mla_qc_norm, mla_kvc_norm, mla_w_uq,
           mla_w_ukv, mla_q_norm, mla_k_norm, diff_q_norm, diff_k_norm, diff_lambda, diff_out_norm,
           branch_proj, w_out):
    B, N, D = x.shape
    NC = ctx.shape[1]
    L = w_mod.shape[0]
    W = lru_conv_w.shape[-1]
    QR, KR = mla_qc_norm.shape[-1], mla_kvc_norm.shape[-1]
    dqw = DIFF_HEADS * 2 * DIFF_D

    sizes = dict(rx=W, rg=W, qc=QR, kvc=KR, kr=MLA_ROPE, dq=dqw, dk=dqw, dv=DIFF_HEADS * DIFF_V,
                 gt=N_BRANCH * D)
    src, off = {}, 0
    for name in ("rx", "rg", "qc", "kvc", "kr", "dq", "dk", "dv", "gt"):
        src[name] = off
        off += sizes[name]
    cols, off = _in_proj_layout(D, W)
    for name, width in (("qc", QR), ("kvc", KR), ("kr", LANE)):
        off = -(-off // width) * width
        cols[name] = off
        off += width
    NP = -(-off // 2048) * 2048 if off > 2048 else off

    def pad_w_in(w):
        out = jnp.zeros((D, NP), BF16)
        for name in sizes:
            out = lax.dynamic_update_slice(out, w[:, src[name]:src[name] + sizes[name]].astype(BF16),
                                           (0, cols[name]))
        return out

    def pad_heads(w, width):
        K = w.shape[0]
        w3 = w.reshape(K, MLA_HEADS, width)
        return jnp.pad(w3, ((0, 0), (0, 0), (0, MLA_QK_PAD - width))).reshape(K, MLA_HEADS * MLA_QK_PAD)

    R = -(-(B + 1) // SUBLANE) * SUBLANE
    cpad = jnp.zeros((R, D), F32).at[:B].set(c).at[B].set(c_ctx)
    mod = _compute_mod(cpad, w_mod, b_mod).reshape(L, R, N_MOD, D)

    tabs_l = _rope_tables(N, identity=False)
    tabs_c = _rope_tables(NC, identity=True)

    xl, xc = x, ctx
    for l in range(L):
        last = l == L - 1
        lam_init = 0.8 - 0.6 * math.exp(-0.3 * l)
        mod_l = mod[l, :B]
        mod_c = jnp.broadcast_to(mod[l, B:B + 1], (B, N_MOD, D))

        def rows(m, s):
            return m[:, 3 * s:3 * s + 1], m[:, 3 * s + 1:3 * s + 2], m[:, 3 * s + 2:3 * s + 3]

        ng = norm_g[l]
        f1 = (ffn_w_gate[l, 0].astype(BF16), ffn_w_up[l, 0].astype(BF16), ffn_w_down[l, 0].astype(BF16))
        f2 = (ffn_w_gate[l, 1].astype(BF16), ffn_w_up[l, 1].astype(BF16), ffn_w_down[l, 1].astype(BF16))
        w_in_p = pad_w_in(w_in[l])
        wuq = pad_heads(mla_w_uq[l], MLA_QK).astype(BF16)
        wukv = mla_w_ukv[l].astype(BF16)
        gq = jnp.pad(mla_q_norm[l], (0, MLA_QK_PAD - MLA_QK)).reshape(1, MLA_QK_PAD)
        gk = jnp.pad(mla_k_norm[l], (0, MLA_QK_PAD - MLA_QK)).reshape(1, MLA_QK_PAD)
        gqc = mla_qc_norm[l].reshape(1, QR)
        gkvc = mla_kvc_norm[l].reshape(1, KR)
        gdq = jnp.tile(diff_q_norm[l], 2).reshape(1, LANE)
        gdk = jnp.tile(diff_k_norm[l], 2).reshape(1, LANE)
        gdo = diff_out_norm[l].reshape(1, DIFF_V)
        bproj = branch_proj[l].astype(BF16)
        wo = w_out[l].astype(BF16)
        lru = dict(cw=lru_conv_w[l], cb=lru_conv_b[l].reshape(1, W))

        m0, m1, m2 = rows(mod_l, 0)
        xl = _ffn(xl, ng[0:1], m0, m1, m2, *f1)
        m0, m1, m2 = rows(mod_c, 0)
        xc = _ffn(xc, ng[0:1], m0, m1, m2, *f1)

        m0, m1, gate_l = rows(mod_l, 1)
        proj_l = _inproj(xl, ng[1:2], m0, m1, w_in_p)
        m0, m1, gate_c = rows(mod_c, 1)
        proj_c = _inproj(xc, ng[1:2], m0, m1, w_in_p)

        h_zero = jnp.zeros((B, 1, W), F32)
        hs_l, hs_c = [], []
        for d, rev in enumerate((False, True)):
            prm = dict(cw=lru["cw"], cb=lru["cb"], wa=lru_wa[l, d].astype(BF16), ba=lru_ba[l, d].reshape(1, W),
                       wx=lru_wx[l, d].astype(BF16), bx=lru_bx[l, d].reshape(1, W),
                       lam=lru_lambda[l, d].reshape(1, W))
            hc, hc_fin = _lru_scan(proj_c, cols["rx"] // W, h_zero, reverse=rev, W=W, **prm)
            hl, _ = _lru_scan(proj_l, cols["rx"] // W, hc_fin, reverse=rev, W=W, **prm)
            hs_l.append(hl)
            hs_c.append(hc)
        a_lat = _lru_out(hs_l[0], hs_l[1], proj_l, cols["rg"] // W)

        q_ml, k_ml, v_ml = _mla_prep(proj_l, cols, gqc, gkvc, wuq, wukv, gq, gk, *tabs_l)
        q_mc, k_mc, v_mc = _mla_prep(proj_c, cols, gqc, gkvc, wuq, wukv, gq, gk, *tabs_c)
        b_lat = _mla_attn(q_ml, [(k_ml, v_ml), (k_mc, v_mc)])

        q_dl, k_dl, v_dl = _diff_prep(proj_l, cols, gdq, gdk, *tabs_l)
        q_dc, k_dc, v_dc = _diff_prep(proj_c, cols, gdq, gdk, *tabs_c)
        c_lat = _diff_attn(q_dl, [(k_dl, v_dl), (k_dc, v_dc)], diff_lambda[l], gdo, lam_init)

        y_lat = _merge(a_lat, b_lat, c_lat, proj_l, bproj)
        xl = _outproj(xl, y_lat, wo, gate_l)

        m0, m1, m2 = rows(mod_l, 2)
        xl = _ffn(xl, ng[2:3], m0, m1, m2, *f2)

        if not last:
            a_ctx = _lru_out(hs_c[0], hs_c[1], proj_c, cols["rg"] // W)
            b_ctx = _mla_attn(q_mc, [(k_mc, v_mc)])
            c_ctx_o = _diff_attn(q_dc, [(k_dc, v_dc)], diff_lambda[l], gdo, lam_init)
            y_ctx = _merge(a_ctx, b_ctx, c_ctx_o, proj_c, bproj)
            xc = _outproj(xc, y_ctx, wo, gate_c)
            m0, m1, m2 = rows(mod_c, 2)
            xc = _ffn(xc, ng[2:3], m0, m1, m2, *f2)
    return xl
```

```python
import functools
import math

import jax
import jax.numpy as jnp
from jax import lax
from jax.experimental import pallas as pl
from jax.experimental.pallas import tpu as pltpu

F32 = jnp.float32
BF16 = jnp.bfloat16

EPS = 1e-6
ROPE_THETA = 10000.0
GRID_W = 64
N_BRANCH = 3
N_MOD = 9
LRU_BLOCKS = 8
CONV_W = 4
LRU_C = 8.0
MLA_HEADS = 8
MLA_NOPE = 128
MLA_ROPE = 64
MLA_V = 128
MLA_QK = MLA_NOPE + MLA_ROPE
DIFF_HEADS = 8
DIFF_D = 64
DIFF_V = 2 * DIFF_D

LANE = 128
SUBLANE = 8
MLA_QK_PAD = 256
VMEM_LIMIT = 56 * 1024 * 1024
LOG2E = math.log2(math.e)
ATTN_TK = 1024
ATTN_UNROLL = 4
MAX_SCORE_BOUND = 30.0


def _cparams(*sem):
    return pltpu.CompilerParams(dimension_semantics=sem, vmem_limit_bytes=VMEM_LIMIT)


def _tile(n, pref):
    if n <= pref:
        return n
    t = pref - pref % SUBLANE
    while t > SUBLANE and n % t:
        t -= SUBLANE
    assert n % t == 0, (n, pref)
    return t


def _norm_mod(x, g, m0, m1):
    ms = jnp.mean(x * x, axis=-1, keepdims=True)
    hn = (x * lax.rsqrt(ms + EPS)) * g
    return hn * (1.0 + m0) + m1


def _mod_kernel(c_ref, w_ref, b_ref, o_ref):
    c = c_ref[...]
    sc = c * jax.nn.sigmoid(c)
    o_ref[0] = jnp.dot(sc, w_ref[0], preferred_element_type=F32) + b_ref[0]


def _compute_mod(cpad, w_mod, b_mod):
    L, D, ND = w_mod.shape
    R = cpad.shape[0]
    tn = _tile(ND, 1024)
    return pl.pallas_call(
        _mod_kernel,
        grid=(L, ND // tn),
        in_specs=[pl.BlockSpec((R, D), lambda l, j: (0, 0)),
                  pl.BlockSpec((1, D, tn), lambda l, j: (l, 0, j)),
                  pl.BlockSpec((1, 1, tn), lambda l, j: (l, 0, j))],
        out_specs=pl.BlockSpec((1, R, tn), lambda l, j: (l, 0, j)),
        out_shape=jax.ShapeDtypeStruct((L, R, ND), F32),
        compiler_params=_cparams("parallel", "parallel"),
        name="adaln_mod",
    )(cpad, w_mod, b_mod.reshape(L, 1, ND))


def _ffn_kernel(x_ref, g_ref, m0_ref, m1_ref, m2_ref, wg_ref, wu_ref, wd_ref, o_ref, h_scr):
    f = pl.program_id(2)

    @pl.when(f == 0)
    def _():
        h_scr[...] = _norm_mod(x_ref[0], g_ref[...], m0_ref[0], m1_ref[0]).astype(BF16)
        o_ref[0] = jnp.zeros(o_ref.shape[1:], F32)

    h = h_scr[...]
    g = jnp.dot(h, wg_ref[...], preferred_element_type=F32)
    u = jnp.dot(h, wu_ref[...], preferred_element_type=F32)
    a = (g * jax.nn.sigmoid(g)) * u
    o_ref[0] += jnp.dot(a.astype(BF16), wd_ref[...], preferred_element_type=F32)

    @pl.when(f == pl.num_programs(2) - 1)
    def _():
        o_ref[0] = x_ref[0] + (0.5 * m2_ref[0]) * o_ref[0]


def _ffn(x, g, m0, m1, m2, wg, wu, wd):
    B, N, D = x.shape
    F = wg.shape[1]
    tm = _tile(N, 512)
    fk = _tile(F, 512)
    tok = pl.BlockSpec((1, tm, D), lambda b, i, f: (b, i, 0))
    row = pl.BlockSpec((1, 1, D), lambda b, i, f: (b, 0, 0))
    return pl.pallas_call(
        _ffn_kernel,
        grid=(B, N // tm, F // fk),
        in_specs=[tok, pl.BlockSpec((1, D), lambda b, i, f: (0, 0)), row, row, row,
                  pl.BlockSpec((D, fk), lambda b, i, f: (0, f)),
                  pl.BlockSpec((D, fk), lambda b, i, f: (0, f)),
                  pl.BlockSpec((fk, D), lambda b, i, f: (f, 0))],
        out_specs=tok,
        out_shape=jax.ShapeDtypeStruct((B, N, D), F32),
        scratch_shapes=[pltpu.VMEM((tm, D), BF16)],
        compiler_params=_cparams("parallel", "parallel", "arbitrary"),
        name="swiglu_half_step",
    )(x, g, m0, m1, m2, wg, wu, wd)


def _inproj_kernel(x_ref, g_ref, m0_ref, m1_ref, w_ref, o_ref, h_scr):
    @pl.when(pl.program_id(2) == 0)
    def _():
        h_scr[...] = _norm_mod(x_ref[0], g_ref[...], m0_ref[0], m1_ref[0]).astype(BF16)

    o_ref[0] = jnp.dot(h_scr[...], w_ref[...], preferred_element_type=F32)


def _inproj(x, g, m0, m1, w):
    B, N, D = x.shape
    NP = w.shape[1]
    tm = _tile(N, 512)
    tn = _tile(NP, 2048)
    row = pl.BlockSpec((1, 1, D), lambda b, i, j: (b, 0, 0))
    return pl.pallas_call(
        _inproj_kernel,
        grid=(B, N // tm, NP // tn),
        in_specs=[pl.BlockSpec((1, tm, D), lambda b, i, j: (b, i, 0)),
                  pl.BlockSpec((1, D), lambda b, i, j: (0, 0)), row, row,
                  pl.BlockSpec((D, tn), lambda b, i, j: (0, j))],
        out_specs=pl.BlockSpec((1, tm, tn), lambda b, i, j: (b, i, j)),
        out_shape=jax.ShapeDtypeStruct((B, N, NP), F32),
        scratch_shapes=[pltpu.VMEM((tm, D), BF16)],
        compiler_params=_cparams("parallel", "parallel", "arbitrary"),
        name="mixer_in_proj",
    )(x, g, m0, m1, w)


def _lru_kernel(x_ref, xp_ref, xn_ref, cw_ref, cb_ref, wa_ref, ba_ref, wx_ref, bx_ref, lam_ref, h0_ref,
                h_ref, hfin_ref, xs_scr, a_scr, u_scr, carry_scr, *, reverse, nt, tm, W):
    i = pl.program_id(1)
    ti = (nt - 1 - i) if reverse else i
    H = SUBLANE

    @pl.when(i == 0)
    def _():
        carry_scr[...] = h0_ref[0]

    zero = jnp.zeros((H, W), F32)
    xs_scr[0:H, :] = jnp.where(ti > 0, xp_ref[0], zero)
    xs_scr[H:H + tm, :] = x_ref[0]
    xs_scr[H + tm:2 * H + tm, :] = jnp.where(ti < nt - 1, xn_ref[0], zero)

    sp = jax.nn.softplus(-lam_ref[...])
    lb = W // LRU_BLOCKS
    left = CONV_W // 2
    for n in range(LRU_BLOCKS):
        sl = slice(n * lb, (n + 1) * lb)
        xc = cb_ref[:, sl] + xs_scr[H - left:H - left + tm, sl] * cw_ref[0:1, sl]
        for j in range(1, CONV_W):
            xc = xc + xs_scr[H - left + j:H - left + j + tm, sl] * cw_ref[j:j + 1, sl]
        xb = xc.astype(BF16)
        r = jax.nn.sigmoid(jnp.dot(xb, wa_ref[n], preferred_element_type=F32) + ba_ref[:, sl])
        ig = jax.nn.sigmoid(jnp.dot(xb, wx_ref[n], preferred_element_type=F32) + bx_ref[:, sl])
        log_a = (-LRU_C) * r * sp[:, sl]
        a_scr[:, sl] = jnp.exp(log_a)
        u_scr[:, sl] = jnp.sqrt(1.0 - jnp.exp(2.0 * log_a)) * (ig * xc)

    rowi = lax.broadcasted_iota(jnp.int32, (H, W), 0)
    ng = tm // H

    def group(gi, h):
        g = (ng - 1 - gi) if reverse else gi
        r0 = pl.multiple_of(g * H, H)
        a = a_scr[pl.ds(r0, H), :]
        u = u_scr[pl.ds(r0, H), :]
        for s in (1, 2, 4):
            sh = (H - s) if reverse else s
            valid = (rowi < H - s) if reverse else (rowi >= s)
            a_s = pltpu.roll(a, sh, 0)
            u_s = pltpu.roll(u, sh, 0)
            u = jnp.where(valid, a * u_s + u, u)
            a = jnp.where(valid, a * a_s, a)
        hh = a * h + u
        h_ref[0, pl.ds(r0, H), :] = hh
        return hh[0:1, :] if reverse else hh[H - 1:H, :]

    h_last = lax.fori_loop(0, ng, group, carry_scr[...])
    carry_scr[...] = h_last
    hfin_ref[0] = h_last


def _lru_scan(proj, col_blk, h0, cw, cb, wa, ba, wx, bx, lam, *, reverse, W):
    B, N, _ = proj.shape
    tm = _tile(N, 256)
    nt = N // tm
    r8 = tm // SUBLANE
    nb8 = N // SUBLANE

    def t_of(i):
        return (nt - 1 - i) if reverse else i

    vec = pl.BlockSpec((1, W), lambda b, i: (0, 0))
    blk = pl.BlockSpec((LRU_BLOCKS, W // LRU_BLOCKS, W // LRU_BLOCKS), lambda b, i: (0, 0, 0))
    kern = functools.partial(_lru_kernel, reverse=reverse, nt=nt, tm=tm, W=W)
    return pl.pallas_call(
        kern,
        grid=(B, nt),
        in_specs=[pl.BlockSpec((1, tm, W), lambda b, i: (b, t_of(i), col_blk)),
                  pl.BlockSpec((1, SUBLANE, W), lambda b, i: (b, jnp.maximum(t_of(i) * r8 - 1, 0), col_blk)),
                  pl.BlockSpec((1, SUBLANE, W), lambda b, i: (b, jnp.minimum((t_of(i) + 1) * r8, nb8 - 1), col_blk)),
                  pl.BlockSpec((CONV_W, W), lambda b, i: (0, 0)), vec, blk, vec, blk, vec, vec,
                  pl.BlockSpec((1, 1, W), lambda b, i: (b, 0, 0))],
        out_specs=[pl.BlockSpec((1, tm, W), lambda b, i: (b, t_of(i), 0)),
                   pl.BlockSpec((1, 1, W), lambda b, i: (b, 0, 0))],
        out_shape=[jax.ShapeDtypeStruct((B, N, W), F32), jax.ShapeDtypeStruct((B, 1, W), F32)],
        scratch_shapes=[pltpu.VMEM((tm + 2 * SUBLANE, W), F32), pltpu.VMEM((tm, W), F32),
                        pltpu.VMEM((tm, W), F32), pltpu.VMEM((1, W), F32)],
        compiler_params=_cparams("parallel", "arbitrary"),
        name="rglru_rev" if reverse else "rglru_fwd",
    )(proj, proj, proj, cw, cb, wa, ba, wx, bx, lam, h0)


def _lru_out_kernel(hf_ref, hb_ref, g_ref, o_ref):
    o_ref[0] = ((hf_ref[0] + hb_ref[0]) * jax.nn.gelu(g_ref[0])).astype(BF16)


def _lru_out(hf, hb, proj, col_blk):
    B, N, W = hf.shape
    tm = _tile(N, 512)
    tok = pl.BlockSpec((1, tm, W), lambda b, i: (b, i, 0))
    return pl.pallas_call(
        _lru_out_kernel,
        grid=(B, N // tm),
        in_specs=[tok, tok, pl.BlockSpec((1, tm, W), lambda b, i: (b, i, col_blk))],
        out_specs=tok,
        out_shape=jax.ShapeDtypeStruct((B, N, W), BF16),
        compiler_params=_cparams("parallel", "parallel"),
        name="rglru_gate_out",
    )(hf, hb, proj)


def _rope(x, cosf, sa, sb):
    return x * cosf + pltpu.roll(x, LANE - MLA_ROPE // 2, 1) * sa + pltpu.roll(x, MLA_ROPE // 2, 1) * sb


def _mla_prep_kernel(qc_ref, kvc_ref, kr_ref, gqc_ref, gkvc_ref, wuq_ref, wukv_ref, gq_ref, gk_ref,
                     cos_ref, sa_ref, sb_ref, q_ref, k_ref, v_ref):
    cosf, sa, sb = cos_ref[...], sa_ref[...], sb_ref[...]
    scale = MLA_QK ** -0.5 * LOG2E

    qc = qc_ref[0]
    qn = (qc * lax.rsqrt(jnp.mean(qc * qc, axis=-1, keepdims=True) + EPS)) * gqc_ref[...]
    qf = jnp.dot(qn.astype(BF16), wuq_ref[...], preferred_element_type=F32)
    kvc = kvc_ref[0]
    kvn = (kvc * lax.rsqrt(jnp.mean(kvc * kvc, axis=-1, keepdims=True) + EPS)) * gkvc_ref[...]
    kvf = jnp.dot(kvn.astype(BF16), wukv_ref[...], preferred_element_type=F32)

    kr = kr_ref[0]
    ss_kr = jnp.sum(kr * kr, axis=-1, keepdims=True)
    kr_rot = _rope(kr * gk_ref[:, LANE:], cosf, sa, sb)

    for h in range(MLA_HEADS):
        o = h * MLA_QK_PAD
        qa = qf[:, o:o + LANE]
        qb = qf[:, o + LANE:o + 2 * LANE]
        ss = jnp.sum(qa * qa, axis=-1, keepdims=True) + jnp.sum(qb * qb, axis=-1, keepdims=True)
        rinv = lax.rsqrt(ss * (1.0 / MLA_QK) + EPS)
        q_ref[0, h, :, 0:LANE] = ((qa * rinv) * gq_ref[:, 0:LANE] * scale).astype(BF16)
        q_ref[0, h, :, LANE:] = (_rope((qb * rinv) * gq_ref[:, LANE:], cosf, sa, sb) * scale).astype(BF16)

        kn = kvf[:, o:o + LANE]
        ssk = jnp.sum(kn * kn, axis=-1, keepdims=True) + ss_kr
        rk = lax.rsqrt(ssk * (1.0 / MLA_QK) + EPS)
        k_ref[0, h, :, 0:LANE] = ((kn * rk) * gk_ref[:, 0:LANE]).astype(BF16)
        k_ref[0, h, :, LANE:] = (kr_rot * rk).astype(BF16)
        v_ref[0, h] = kvf[:, o + LANE:o + 2 * LANE].astype(BF16)


def _mla_prep(proj, cols, gqc, gkvc, wuq, wukv, gq, gk, cosf, sa, sb):
    B, N, _ = proj.shape
    tm = _tile(N, 256)
    QR, KR = gqc.shape[1], gkvc.shape[1]
    HP = MLA_HEADS * MLA_QK_PAD
    tab = pl.BlockSpec((tm, LANE), lambda b, i: (i, 0))

    def full(a):
        return pl.BlockSpec(a.shape, lambda b, i: (0,) * a.ndim)

    hd = lambda w: pl.BlockSpec((1, MLA_HEADS, tm, w), lambda b, i: (b, 0, i, 0))
    return pl.pallas_call(
        _mla_prep_kernel,
        grid=(B, N // tm),
        in_specs=[pl.BlockSpec((1, tm, QR), lambda b, i: (b, i, cols["qc"] // QR)),
                  pl.BlockSpec((1, tm, KR), lambda b, i: (b, i, cols["kvc"] // KR)),
                  pl.BlockSpec((1, tm, LANE), lambda b, i: (b, i, cols["kr"] // LANE)),
                  full(gqc), full(gkvc), full(wuq), full(wukv), full(gq), full(gk), tab, tab, tab],
        out_specs=[hd(MLA_QK_PAD), hd(MLA_QK_PAD), hd(MLA_V)],
        out_shape=[jax.ShapeDtypeStruct((B, MLA_HEADS, N, MLA_QK_PAD), BF16),
                   jax.ShapeDtypeStruct((B, MLA_HEADS, N, MLA_QK_PAD), BF16),
                   jax.ShapeDtypeStruct((B, MLA_HEADS, N, MLA_V), BF16)],
        compiler_params=_cparams("parallel", "parallel"),
        name="mla_qkv_prep",
    )(proj, proj, proj, gqc, gkvc, wuq, wukv, gq, gk, cosf, sa, sb)


def _diff_prep_kernel(dq_ref, dk_ref, dv_ref, gq_ref, gk_ref, cos_ref, sa_ref, sb_ref, q_ref, k_ref, v_ref):
    cosf, sa, sb = cos_ref[...], sa_ref[...], sb_ref[...]
    tm = dq_ref.shape[1]
    first = lax.broadcasted_iota(jnp.int32, (tm, LANE), 1) < DIFF_D
    scale = DIFF_D ** -0.5 * LOG2E

    def norm_rope(x, g):
        xx = x * x
        s0 = jnp.sum(jnp.where(first, xx, 0.0), axis=-1, keepdims=True)
        s1 = jnp.sum(jnp.where(first, 0.0, xx), axis=-1, keepdims=True)
        rinv = jnp.where(first, lax.rsqrt(s0 * (1.0 / DIFF_D) + EPS), lax.rsqrt(s1 * (1.0 / DIFF_D) + EPS))
        return _rope((x * rinv) * g, cosf, sa, sb)

    for h in range(DIFF_HEADS):
        sl = slice(h * LANE, (h + 1) * LANE)
        q_ref[0, h] = (norm_rope(dq_ref[0, :, sl], gq_ref[...]) * scale).astype(BF16)
        k_ref[0, h] = norm_rope(dk_ref[0, :, sl], gk_ref[...]).astype(BF16)
        v_ref[0, h] = dv_ref[0, :, sl].astype(BF16)


def _diff_prep(proj, cols, gq2, gk2, cosf, sa, sb):
    B, N, _ = proj.shape
    tm = _tile(N, 256)
    HW = DIFF_HEADS * LANE
    tab = pl.BlockSpec((tm, LANE), lambda b, i: (i, 0))
    g = pl.BlockSpec((1, LANE), lambda b, i: (0, 0))
    hd = pl.BlockSpec((1, DIFF_HEADS, tm, LANE), lambda b, i: (b, 0, i, 0))
    shp = jax.ShapeDtypeStruct((B, DIFF_HEADS, N, LANE), BF16)
    return pl.pallas_call(
        _diff_prep_kernel,
        grid=(B, N // tm),
        in_specs=[pl.BlockSpec((1, tm, HW), lambda b, i: (b, i, cols["dq"] // HW)),
                  pl.BlockSpec((1, tm, HW), lambda b, i: (b, i, cols["dk"] // HW)),
                  pl.BlockSpec((1, tm, HW), lambda b, i: (b, i, cols["dv"] // HW)),
                  g, g, tab, tab, tab],
        out_specs=[hd, hd, hd],
        out_shape=[shp, shp, shp],
        compiler_params=_cparams("parallel", "parallel"),
        name="diff_qkv_prep",
    )(proj, proj, proj, gq2, gk2, cosf, sa, sb)


def _softmax_step(s, v, shift, m_scr, l_scr, acc_scr):
    tk = s.shape[1]
    if shift is not None:
        p = jnp.exp2(s - shift)
        part = p[:, 0:LANE]
        for c in range(1, tk // LANE):
            part = part + p[:, c * LANE:(c + 1) * LANE]
        l_scr[...] += part
        acc_scr[...] += jnp.dot(p.astype(BF16), v, preferred_element_type=F32)
        return
    m_prev = m_scr[...]
    m_new = jnp.maximum(m_prev, jnp.max(s, axis=1, keepdims=True))
    alpha = jnp.exp2(m_prev - m_new)
    m_b = jnp.concatenate([m_new] * (tk // LANE), axis=1) if tk % LANE == 0 else m_new[:, 0:1]
    p = jnp.exp2(s - m_b)
    l_scr[...] = alpha * l_scr[...] + jnp.sum(p, axis=1, keepdims=True)
    acc_scr[...] = alpha * acc_scr[...] + jnp.dot(p.astype(BF16), v, preferred_element_type=F32)
    m_scr[...] = m_new


def _row_sum(l_scr, bounded):
    return jnp.sum(l_scr[...], axis=1, keepdims=True) if bounded else l_scr[...]


def _qk(q, k):
    return lax.dot_general(q, k, (((1,), (1,)), ((), ())), preferred_element_type=F32)


def _kv_loop(kv_refs, tk, step, unroll):
    for c in range(len(kv_refs) // 2):
        k_ref, v_ref = kv_refs[2 * c], kv_refs[2 * c + 1]
        nk = k_ref.shape[2]
        t = _tile(nk, tk)

        def body(j, carry, k_ref=k_ref, v_ref=v_ref, t=t):
            r0 = pl.multiple_of(j * t, t)
            step(k_ref[0, 0, pl.ds(r0, t), :], v_ref[0, 0, pl.ds(r0, t), :])
            return carry

        lax.fori_loop(0, nk // t, body, 0, unroll=math.gcd(nk // t, unroll))


def _mla_attn_kernel(*refs, n_kv, tk, bounded):
    shift_ref, q_ref = refs[0], refs[1]
    kv_refs = refs[2:2 + 2 * n_kv]
    o_ref = refs[2 + 2 * n_kv]
    m_scr, l_scr, acc_scr = refs[3 + 2 * n_kv:]
    if not bounded:
        m_scr[...] = jnp.full(m_scr.shape, -jnp.inf, F32)
    l_scr[...] = jnp.zeros(l_scr.shape, F32)
    acc_scr[...] = jnp.zeros(acc_scr.shape, F32)
    q = q_ref[0, 0]
    shift = shift_ref[0] if bounded else None
    _kv_loop(kv_refs, tk, lambda k, v: _softmax_step(_qk(q, k), v, shift, m_scr, l_scr, acc_scr),
             ATTN_UNROLL if bounded else 1)
    o_ref[0] = (acc_scr[...] / _row_sum(l_scr, bounded)).astype(BF16)


def _mla_attn(q, kvs, shift, bounded):
    B, H, N, DQ = q.shape
    tq = _tile(N, 1024)
    in_specs = [pl.BlockSpec(memory_space=pltpu.SMEM),
                pl.BlockSpec((1, 1, tq, DQ), lambda b, h, i: (b, h, i, 0))]
    args = [shift, q]
    for k, v in kvs:
        nk = k.shape[2]
        in_specs += [pl.BlockSpec((1, 1, nk, DQ), lambda b, h, i: (b, h, 0, 0)),
                     pl.BlockSpec((1, 1, nk, MLA_V), lambda b, h, i: (b, h, 0, 0))]
        args += [k, v]
    stat = pltpu.VMEM((tq, LANE), F32)
    return pl.pallas_call(
        functools.partial(_mla_attn_kernel, n_kv=len(kvs), tk=ATTN_TK, bounded=bounded),
        grid=(B, H, N // tq),
        in_specs=in_specs,
        out_specs=pl.BlockSpec((1, tq, MLA_V), lambda b, h, i: (b, i, h)),
        out_shape=jax.ShapeDtypeStruct((B, N, H * MLA_V), BF16),
        scratch_shapes=[stat, stat, stat],
        compiler_params=_cparams("parallel", "parallel", "arbitrary"),
        name="mla_attention",
    )(*args)


def _diff_attn_kernel(*refs, n_kv, tk, lam_init, bounded):
    shift_ref, q_ref = refs[0], refs[1]
    kv_refs = refs[2:2 + 2 * n_kv]
    lamp_ref, gout_ref, o_ref = refs[2 + 2 * n_kv:5 + 2 * n_kv]
    m0, l0, acc0, m1, l1, acc1 = refs[5 + 2 * n_kv:]
    if not bounded:
        for m in (m0, m1):
            m[...] = jnp.full(m.shape, -jnp.inf, F32)
    for z in (l0, l1, acc0, acc1):
        z[...] = jnp.zeros(z.shape, F32)
    q = q_ref[0, 0]
    first = lax.broadcasted_iota(jnp.int32, q.shape, 1) < DIFF_D
    zero = jnp.zeros(q.shape, BF16)
    qa = jnp.where(first, q, zero)
    qb = jnp.where(first, zero, q)
    shift = shift_ref[0] if bounded else None

    def step(k, v):
        _softmax_step(_qk(qa, k), v, shift, m0, l0, acc0)
        _softmax_step(_qk(qb, k), v, shift, m1, l1, acc1)

    _kv_loop(kv_refs, tk, step, ATTN_UNROLL if bounded else 1)

    lp = lamp_ref[...]
    lam = (jnp.exp(jnp.sum(lp[0:1] * lp[1:2], keepdims=True))
           - jnp.exp(jnp.sum(lp[2:3] * lp[3:4], keepdims=True)) + lam_init)
    o = acc0[...] / _row_sum(l0, bounded) - lam * (acc1[...] / _row_sum(l1, bounded))
    on = (o * lax.rsqrt(jnp.mean(o * o, axis=-1, keepdims=True) + EPS)) * gout_ref[...]
    o_ref[0] = (on * (1.0 - lam_init)).astype(BF16)


def _diff_attn(q, kvs, lam_p, g_out, lam_init, shift, bounded):
    B, H, N, DQ = q.shape
    tq = _tile(N, 1024)
    in_specs = [pl.BlockSpec(memory_space=pltpu.SMEM),
                pl.BlockSpec((1, 1, tq, DQ), lambda b, h, i: (b, h, i, 0))]
    args = [shift, q]
    for k, v in kvs:
        nk = k.shape[2]
        kv_spec = pl.BlockSpec((1, 1, nk, LANE), lambda b, h, i: (b, h, 0, 0))
        in_specs += [kv_spec, kv_spec]
        args += [k, v]
    in_specs += [pl.BlockSpec(lam_p.shape, lambda b, h, i: (0, 0)),
                 pl.BlockSpec((1, DIFF_V), lambda b, h, i: (0, 0))]
    args += [lam_p, g_out]
    stat = pltpu.VMEM((tq, LANE), F32)
    return pl.pallas_call(
        functools.partial(_diff_attn_kernel, n_kv=len(kvs), tk=ATTN_TK, lam_init=lam_init, bounded=bounded),
        grid=(B, H, N // tq),
        in_specs=in_specs,
        out_specs=pl.BlockSpec((1, tq, DIFF_V), lambda b, h, i: (b, i, h)),
        out_shape=jax.ShapeDtypeStruct((B, N, H * DIFF_V), BF16),
        scratch_shapes=[stat] * 6,
        compiler_params=_cparams("parallel", "parallel", "arbitrary"),
        name="diff_attention",
    )(*args)


def _merge_kernel(a_ref, b_ref, c_ref, gt_ref, p_ref, o_ref, acc_scr):
    j = pl.program_id(2)

    def contrib(br_ref):
        return jax.nn.sigmoid(gt_ref[0]) * jnp.dot(br_ref[0], p_ref[0], preferred_element_type=F32)

    @pl.when(j == 0)
    def _():
        acc_scr[...] = contrib(a_ref)

    @pl.when(j == 1)
    def _():
        acc_scr[...] += contrib(b_ref)

    @pl.when(j == 2)
    def _():
        o_ref[0] = (acc_scr[...] + contrib(c_ref)).astype(BF16)


def _merge(a, b, c, proj, bproj):
    B, N, W = a.shape
    D = bproj.shape[2]
    tm = _tile(N, 512)
    br = pl.BlockSpec((1, tm, W), lambda b_, i, j: (b_, i, 0))
    return pl.pallas_call(
        _merge_kernel,
        grid=(B, N // tm, N_BRANCH),
        in_specs=[br, br, br,
                  pl.BlockSpec((1, tm, D), lambda b_, i, j: (b_, i, j)),
                  pl.BlockSpec((1, W, D), lambda b_, i, j: (j, 0, 0))],
        out_specs=pl.BlockSpec((1, tm, D), lambda b_, i, j: (b_, i, 0)),
        out_shape=jax.ShapeDtypeStruct((B, N, D), BF16),
        scratch_shapes=[pltpu.VMEM((tm, D), F32)],
        compiler_params=_cparams("parallel", "parallel", "arbitrary"),
        name="branch_merge",
    )(a, b, c, proj, bproj)


def _outproj_kernel(x_ref, y_ref, w_ref, m2_ref, o_ref):
    o_ref[0] = x_ref[0] + m2_ref[0] * jnp.dot(y_ref[0], w_ref[...], preferred_element_type=F32)


def _outproj(x, y, w, m2):
    B, N, D = x.shape
    tm = _tile(N, 512)
    tok = pl.BlockSpec((1, tm, D), lambda b, i: (b, i, 0))
    return pl.pallas_call(
        _outproj_kernel,
        grid=(B, N // tm),
        in_specs=[tok, tok, pl.BlockSpec((D, D), lambda b, i: (0, 0)),
                  pl.BlockSpec((1, 1, D), lambda b, i: (b, 0, 0))],
        out_specs=tok,
        out_shape=jax.ShapeDtypeStruct((B, N, D), F32),
        compiler_params=_cparams("parallel", "parallel"),
        name="mixer_out_proj",
    )(x, y, w, m2)


def _rope_tables(n_tokens, identity):
    half = MLA_ROPE // 2
    if identity:
        return (jnp.ones((n_tokens, LANE), F32), jnp.zeros((n_tokens, LANE), F32),
                jnp.zeros((n_tokens, LANE), F32))
    rows = n_tokens // GRID_W
    axis_dim = MLA_ROPE // 2
    inv_freq = ROPE_THETA ** (-jnp.arange(0, axis_dim, 2, dtype=F32) / axis_dim)
    nf = axis_dim // 2
    row_ang = jnp.arange(rows, dtype=F32)[:, None] * inv_freq
    col_ang = jnp.arange(GRID_W, dtype=F32)[:, None] * inv_freq
    ang = jnp.concatenate([jnp.broadcast_to(row_ang[:, None, :], (rows, GRID_W, nf)),
                           jnp.broadcast_to(col_ang[None, :, :], (rows, GRID_W, nf))], axis=-1)
    ang = ang.reshape(rows * GRID_W, axis_dim)
    cos, sin = jnp.cos(ang), jnp.sin(ang)
    z = jnp.zeros_like(sin)
    assert half == axis_dim
    return (jnp.concatenate([cos, cos, cos, cos], axis=1),
            jnp.concatenate([-sin, z, -sin, z], axis=1),
            jnp.concatenate([z, sin, z, sin], axis=1))


def _in_proj_layout(D, W):
    order = [("gt", N_BRANCH * D), ("rx", W), ("rg", W), ("dq", DIFF_HEADS * 2 * DIFF_D),
             ("dk", DIFF_HEADS * 2 * DIFF_D), ("dv", DIFF_HEADS * DIFF_V)]
    cols, off = {}, 0
    for name, width in order:
        if name != "gt":
            off = -(-off // width) * width
        cols[name] = off
        off += width
    return cols, off


def kernel(x, c, ctx, c_ctx, w_mod, b_mod, norm_g, ffn_w_gate, ffn_w_up, ffn_w_down, w_in, lru_conv_w,
           lru_conv_b, lru_wa, lru_ba, lru_wx, lru_bx, lru_lambda, mla_qc_norm, mla_kvc_norm, mla_w_uq,
           mla_w_ukv, mla_q_norm, mla_k_norm, diff_q_norm, diff_k_norm, diff_lambda, diff_out_norm,
           branch_proj, w_out):
    B, N, D = x.shape
    NC = ctx.shape[1]
    L = w_mod.shape[0]
    W = lru_conv_w.shape[-1]
    QR, KR = mla_qc_norm.shape[-1], mla_kvc_norm.shape[-1]
    dqw = DIFF_HEADS * 2 * DIFF_D

    sizes = dict(rx=W, rg=W, qc=QR, kvc=KR, kr=MLA_ROPE, dq=dqw, dk=dqw, dv=DIFF_HEADS * DIFF_V,
                 gt=N_BRANCH * D)
    src, off = {}, 0
    for name in ("rx", "rg", "qc", "kvc", "kr", "dq", "dk", "dv", "gt"):
        src[name] = off
        off += sizes[name]
    cols, off = _in_proj_layout(D, W)
    for name, width in (("qc", QR), ("kvc", KR), ("kr", LANE)):
        off = -(-off // width) * width
        cols[name] = off
        off += width
    NP = -(-off // 2048) * 2048 if off > 2048 else off

    def pad_w_in(w):
        out = jnp.zeros((D, NP), BF16)
        for name in sizes:
            out = lax.dynamic_update_slice(out, w[:, src[name]:src[name] + sizes[name]].astype(BF16),
                                           (0, cols[name]))
        return out

    def pad_heads(w, width):
        K = w.shape[0]
        w3 = w.reshape(K, MLA_HEADS, width)
        return jnp.pad(w3, ((0, 0), (0, 0), (0, MLA_QK_PAD - width))).reshape(K, MLA_HEADS * MLA_QK_PAD)

    R = -(-(B + 1) // SUBLANE) * SUBLANE
    cpad = jnp.zeros((R, D), F32).at[:B].set(c).at[B].set(c_ctx)
    mod = _compute_mod(cpad, w_mod, b_mod).reshape(L, R, N_MOD, D)

    tabs_l = _rope_tables(N, identity=False)
    tabs_c = _rope_tables(NC, identity=True)

    xl, xc = x, ctx
    for l in range(L):
        last = l == L - 1
        lam_init = 0.8 - 0.6 * math.exp(-0.3 * l)
        mod_l = mod[l, :B]
        mod_c = jnp.broadcast_to(mod[l, B:B + 1], (B, N_MOD, D))

        def rows(m, s):
            return m[:, 3 * s:3 * s + 1], m[:, 3 * s + 1:3 * s + 2], m[:, 3 * s + 2:3 * s + 3]

        ng = norm_g[l]
        f1 = (ffn_w_gate[l, 0].astype(BF16), ffn_w_up[l, 0].astype(BF16), ffn_w_down[l, 0].astype(BF16))
        f2 = (ffn_w_gate[l, 1].astype(BF16), ffn_w_up[l, 1].astype(BF16), ffn_w_down[l, 1].astype(BF16))
        w_in_p = pad_w_in(w_in[l])
        wuq = pad_heads(mla_w_uq[l], MLA_QK).astype(BF16)
        wukv = mla_w_ukv[l].astype(BF16)
        gq = jnp.pad(mla_q_norm[l], (0, MLA_QK_PAD - MLA_QK)).reshape(1, MLA_QK_PAD)
        gk = jnp.pad(mla_k_norm[l], (0, MLA_QK_PAD - MLA_QK)).reshape(1, MLA_QK_PAD)
        gqc = mla_qc_norm[l].reshape(1, QR)
        gkvc = mla_kvc_norm[l].reshape(1, KR)
        gdq = jnp.tile(diff_q_norm[l], 2).reshape(1, LANE)
        gdk = jnp.tile(diff_k_norm[l], 2).reshape(1, LANE)
        gdo = diff_out_norm[l].reshape(1, DIFF_V)
        bproj = branch_proj[l].astype(BF16)
        wo = w_out[l].astype(BF16)
        lru = dict(cw=lru_conv_w[l], cb=lru_conv_b[l].reshape(1, W))

        bound_m = math.sqrt(MLA_QK) * jnp.max(jnp.abs(mla_q_norm[l])) * jnp.max(jnp.abs(mla_k_norm[l]))
        bound_d = math.sqrt(DIFF_D) * jnp.max(jnp.abs(diff_q_norm[l])) * jnp.max(jnp.abs(diff_k_norm[l]))

        def mla_attend(q, kvs):
            shift = (bound_m * LOG2E).reshape(1)
            return lax.cond(bound_m <= MAX_SCORE_BOUND,
                            lambda: _mla_attn(q, kvs, shift, True), lambda: _mla_attn(q, kvs, shift, False))

        def diff_attend(q, kvs):
            shift = (bound_d * LOG2E).reshape(1)
            args = (diff_lambda[l], gdo, lam_init, shift)
            return lax.cond(bound_d <= MAX_SCORE_BOUND,
                            lambda: _diff_attn(q, kvs, *args, True), lambda: _diff_attn(q, kvs, *args, False))

        m0, m1, m2 = rows(mod_l, 0)
        xl = _ffn(xl, ng[0:1], m0, m1, m2, *f1)
        m0, m1, m2 = rows(mod_c, 0)
        xc = _ffn(xc, ng[0:1], m0, m1, m2, *f1)

        m0, m1, gate_l = rows(mod_l, 1)
        proj_l = _inproj(xl, ng[1:2], m0, m1, w_in_p)
        m0, m1, gate_c = rows(mod_c, 1)
        proj_c = _inproj(xc, ng[1:2], m0, m1, w_in_p)

        h_zero = jnp.zeros((B, 1, W), F32)
        hs_l, hs_c = [], []
        for d, rev in enumerate((False, True)):
            prm = dict(cw=lru["cw"], cb=lru["cb"], wa=lru_wa[l, d].astype(BF16), ba=lru_ba[l, d].reshape(1, W),
                       wx=lru_wx[l, d].astype(BF16), bx=lru_bx[l, d].reshape(1, W),
                       lam=lru_lambda[l, d].reshape(1, W))
            hc, hc_fin = _lru_scan(proj_c, cols["rx"] // W, h_zero, reverse=rev, W=W, **prm)
            hl, _ = _lru_scan(proj_l, cols["rx"] // W, hc_fin, reverse=rev, W=W, **prm)
            hs_l.append(hl)
            hs_c.append(hc)
        a_lat = _lru_out(hs_l[0], hs_l[1], proj_l, cols["rg"] // W)

        q_ml, k_ml, v_ml = _mla_prep(proj_l, cols, gqc, gkvc, wuq, wukv, gq, gk, *tabs_l)
        q_mc, k_mc, v_mc = _mla_prep(proj_c, cols, gqc, gkvc, wuq, wukv, gq, gk, *tabs_c)
        b_lat = mla_attend(q_ml, [(k_ml, v_ml), (k_mc, v_mc)])

        q_dl, k_dl, v_dl = _diff_prep(proj_l, cols, gdq, gdk, *tabs_l)
        q_dc, k_dc, v_dc = _diff_prep(proj_c, cols, gdq, gdk, *tabs_c)
        c_lat = diff_attend(q_dl, [(k_dl, v_dl), (k_dc, v_dc)])

        y_lat = _merge(a_lat, b_lat, c_lat, proj_l, bproj)
        xl = _outproj(xl, y_lat, wo, gate_l)

        m0, m1, m2 = rows(mod_l, 2)
        xl = _ffn(xl, ng[2:3], m0, m1, m2, *f2)

        if not last:
            a_ctx = _lru_out(hs_c[0], hs_c[1], proj_c, cols["rg"] // W)
            b_ctx = mla_attend(q_mc, [(k_mc, v_mc)])
            c_ctx_o = diff_attend(q_dc, [(k_dc, v_dc)])
            y_ctx = _merge(a_ctx, b_ctx, c_ctx_o, proj_c, bproj)
            xc = _outproj(xc, y_ctx, wo, gate_c)
            m0, m1, m2 = rows(mod_c, 2)
            xc = _ffn(xc, ng[2:3], m0, m1, m2, *f2)
    return xl
```

```python
import functools
import math

import jax
import jax.numpy as jnp
from jax import lax
from jax.experimental import pallas as pl
from jax.experimental.pallas import tpu as pltpu

F32 = jnp.float32
BF16 = jnp.bfloat16

EPS = 1e-6
ROPE_THETA = 10000.0
GRID_W = 64
N_BRANCH = 3
N_MOD = 9
LRU_BLOCKS = 8
CONV_W = 4
LRU_C = 8.0
MLA_HEADS = 8
MLA_NOPE = 128
MLA_ROPE = 64
MLA_V = 128
MLA_QK = MLA_NOPE + MLA_ROPE
DIFF_HEADS = 8
DIFF_D = 64
DIFF_V = 2 * DIFF_D

LANE = 128
SUBLANE = 8
MLA_QK_PAD = 256
VMEM_LIMIT = 56 * 1024 * 1024
LOG2E = math.log2(math.e)
ATTN_TK = 1024
ATTN_UNROLL = 8
MAX_SCORE_BOUND = 30.0


def _cparams(*sem):
    return pltpu.CompilerParams(dimension_semantics=sem, vmem_limit_bytes=VMEM_LIMIT)


def _tile(n, pref):
    if n <= pref:
        return n
    t = pref - pref % SUBLANE
    while t > SUBLANE and n % t:
        t -= SUBLANE
    assert n % t == 0, (n, pref)
    return t


def _norm_mod(x, g, m0, m1):
    ms = jnp.mean(x * x, axis=-1, keepdims=True)
    hn = (x * lax.rsqrt(ms + EPS)) * g
    return hn * (1.0 + m0) + m1


def _mod_kernel(c_ref, w_ref, b_ref, o_ref):
    c = c_ref[...]
    sc = c * jax.nn.sigmoid(c)
    o_ref[0] = jnp.dot(sc, w_ref[0], preferred_element_type=F32) + b_ref[0]


def _compute_mod(cpad, w_mod, b_mod):
    L, D, ND = w_mod.shape
    R = cpad.shape[0]
    tn = _tile(ND, 1024)
    return pl.pallas_call(
        _mod_kernel,
        grid=(L, ND // tn),
        in_specs=[pl.BlockSpec((R, D), lambda l, j: (0, 0)),
                  pl.BlockSpec((1, D, tn), lambda l, j: (l, 0, j)),
                  pl.BlockSpec((1, 1, tn), lambda l, j: (l, 0, j))],
        out_specs=pl.BlockSpec((1, R, tn), lambda l, j: (l, 0, j)),
        out_shape=jax.ShapeDtypeStruct((L, R, ND), F32),
        compiler_params=_cparams("parallel", "parallel"),
        name="adaln_mod",
    )(cpad, w_mod, b_mod.reshape(L, 1, ND))


def _ffn_kernel(x_ref, g_ref, m0_ref, m1_ref, m2_ref, wg_ref, wu_ref, wd_ref, o_ref, h_scr):
    f = pl.program_id(2)

    @pl.when(f == 0)
    def _():
        h_scr[...] = _norm_mod(x_ref[0], g_ref[...], m0_ref[0], m1_ref[0]).astype(BF16)
        o_ref[0] = jnp.zeros(o_ref.shape[1:], F32)

    h = h_scr[...]
    g = jnp.dot(h, wg_ref[...], preferred_element_type=F32)
    u = jnp.dot(h, wu_ref[...], preferred_element_type=F32)
    a = (g * jax.nn.sigmoid(g)) * u
    o_ref[0] += jnp.dot(a.astype(BF16), wd_ref[...], preferred_element_type=F32)

    @pl.when(f == pl.num_programs(2) - 1)
    def _():
        o_ref[0] = x_ref[0] + (0.5 * m2_ref[0]) * o_ref[0]


def _ffn(x, g, m0, m1, m2, wg, wu, wd):
    B, N, D = x.shape
    F = wg.shape[1]
    tm = _tile(N, 512)
    fk = _tile(F, 512)
    tok = pl.BlockSpec((1, tm, D), lambda b, i, f: (b, i, 0))
    row = pl.BlockSpec((1, 1, D), lambda b, i, f: (b, 0, 0))
    return pl.pallas_call(
        _ffn_kernel,
        grid=(B, N // tm, F // fk),
        in_specs=[tok, pl.BlockSpec((1, D), lambda b, i, f: (0, 0)), row, row, row,
                  pl.BlockSpec((D, fk), lambda b, i, f: (0, f)),
                  pl.BlockSpec((D, fk), lambda b, i, f: (0, f)),
                  pl.BlockSpec((fk, D), lambda b, i, f: (f, 0))],
        out_specs=tok,
        out_shape=jax.ShapeDtypeStruct((B, N, D), F32),
        scratch_shapes=[pltpu.VMEM((tm, D), BF16)],
        compiler_params=_cparams("parallel", "parallel", "arbitrary"),
        name="swiglu_half_step",
    )(x, g, m0, m1, m2, wg, wu, wd)


def _inproj_kernel(x_ref, g_ref, m0_ref, m1_ref, w_ref, o_ref, h_scr):
    @pl.when(pl.program_id(2) == 0)
    def _():
        h_scr[...] = _norm_mod(x_ref[0], g_ref[...], m0_ref[0], m1_ref[0]).astype(BF16)

    o_ref[0] = jnp.dot(h_scr[...], w_ref[...], preferred_element_type=F32)


def _inproj(x, g, m0, m1, w):
    B, N, D = x.shape
    NP = w.shape[1]
    tm = _tile(N, 512)
    tn = _tile(NP, 2048)
    row = pl.BlockSpec((1, 1, D), lambda b, i, j: (b, 0, 0))
    return pl.pallas_call(
        _inproj_kernel,
        grid=(B, N // tm, NP // tn),
        in_specs=[pl.BlockSpec((1, tm, D), lambda b, i, j: (b, i, 0)),
                  pl.BlockSpec((1, D), lambda b, i, j: (0, 0)), row, row,
                  pl.BlockSpec((D, tn), lambda b, i, j: (0, j))],
        out_specs=pl.BlockSpec((1, tm, tn), lambda b, i, j: (b, i, j)),
        out_shape=jax.ShapeDtypeStruct((B, N, NP), F32),
        scratch_shapes=[pltpu.VMEM((tm, D), BF16)],
        compiler_params=_cparams("parallel", "parallel", "arbitrary"),
        name="mixer_in_proj",
    )(x, g, m0, m1, w)


def _lru_kernel(x_ref, xp_ref, xn_ref, cw_ref, cb_ref, wa_ref, ba_ref, wx_ref, bx_ref, lam_ref, h0_ref,
                h_ref, hfin_ref, xs_scr, a_scr, u_scr, carry_scr, *, reverse, nt, tm, W):
    i = pl.program_id(1)
    ti = (nt - 1 - i) if reverse else i
    H = SUBLANE

    @pl.when(i == 0)
    def _():
        carry_scr[...] = h0_ref[0]

    zero = jnp.zeros((H, W), F32)
    xs_scr[0:H, :] = jnp.where(ti > 0, xp_ref[0], zero)
    xs_scr[H:H + tm, :] = x_ref[0]
    xs_scr[H + tm:2 * H + tm, :] = jnp.where(ti < nt - 1, xn_ref[0], zero)

    sp = jax.nn.softplus(-lam_ref[...])
    lb = W // LRU_BLOCKS
    left = CONV_W // 2
    for n in range(LRU_BLOCKS):
        sl = slice(n * lb, (n + 1) * lb)
        xc = cb_ref[:, sl] + xs_scr[H - left:H - left + tm, sl] * cw_ref[0:1, sl]
        for j in range(1, CONV_W):
            xc = xc + xs_scr[H - left + j:H - left + j + tm, sl] * cw_ref[j:j + 1, sl]
        xb = xc.astype(BF16)
        r = jax.nn.sigmoid(jnp.dot(xb, wa_ref[n], preferred_element_type=F32) + ba_ref[:, sl])
        ig = jax.nn.sigmoid(jnp.dot(xb, wx_ref[n], preferred_element_type=F32) + bx_ref[:, sl])
        log_a = (-LRU_C) * r * sp[:, sl]
        a_scr[:, sl] = jnp.exp(log_a)
        u_scr[:, sl] = jnp.sqrt(1.0 - jnp.exp(2.0 * log_a)) * (ig * xc)

    rowi = lax.broadcasted_iota(jnp.int32, (H, W), 0)
    ng = tm // H

    def group(gi, h):
        g = (ng - 1 - gi) if reverse else gi
        r0 = pl.multiple_of(g * H, H)
        a = a_scr[pl.ds(r0, H), :]
        u = u_scr[pl.ds(r0, H), :]
        for s in (1, 2, 4):
            sh = (H - s) if reverse else s
            valid = (rowi < H - s) if reverse else (rowi >= s)
            a_s = pltpu.roll(a, sh, 0)
            u_s = pltpu.roll(u, sh, 0)
            u = jnp.where(valid, a * u_s + u, u)
            a = jnp.where(valid, a * a_s, a)
        hh = a * h + u
        h_ref[0, pl.ds(r0, H), :] = hh
        return hh[0:1, :] if reverse else hh[H - 1:H, :]

    h_last = lax.fori_loop(0, ng, group, carry_scr[...])
    carry_scr[...] = h_last
    hfin_ref[0] = h_last


def _lru_scan(proj, col_blk, h0, cw, cb, wa, ba, wx, bx, lam, *, reverse, W):
    B, N, _ = proj.shape
    tm = _tile(N, 256)
    nt = N // tm
    r8 = tm // SUBLANE
    nb8 = N // SUBLANE

    def t_of(i):
        return (nt - 1 - i) if reverse else i

    vec = pl.BlockSpec((1, W), lambda b, i: (0, 0))
    blk = pl.BlockSpec((LRU_BLOCKS, W // LRU_BLOCKS, W // LRU_BLOCKS), lambda b, i: (0, 0, 0))
    kern = functools.partial(_lru_kernel, reverse=reverse, nt=nt, tm=tm, W=W)
    return pl.pallas_call(
        kern,
        grid=(B, nt),
        in_specs=[pl.BlockSpec((1, tm, W), lambda b, i: (b, t_of(i), col_blk)),
                  pl.BlockSpec((1, SUBLANE, W), lambda b, i: (b, jnp.maximum(t_of(i) * r8 - 1, 0), col_blk)),
                  pl.BlockSpec((1, SUBLANE, W), lambda b, i: (b, jnp.minimum((t_of(i) + 1) * r8, nb8 - 1), col_blk)),
                  pl.BlockSpec((CONV_W, W), lambda b, i: (0, 0)), vec, blk, vec, blk, vec, vec,
                  pl.BlockSpec((1, 1, W), lambda b, i: (b, 0, 0))],
        out_specs=[pl.BlockSpec((1, tm, W), lambda b, i: (b, t_of(i), 0)),
                   pl.BlockSpec((1, 1, W), lambda b, i: (b, 0, 0))],
        out_shape=[jax.ShapeDtypeStruct((B, N, W), F32), jax.ShapeDtypeStruct((B, 1, W), F32)],
        scratch_shapes=[pltpu.VMEM((tm + 2 * SUBLANE, W), F32), pltpu.VMEM((tm, W), F32),
                        pltpu.VMEM((tm, W), F32), pltpu.VMEM((1, W), F32)],
        compiler_params=_cparams("parallel", "arbitrary"),
        name="rglru_rev" if reverse else "rglru_fwd",
    )(proj, proj, proj, cw, cb, wa, ba, wx, bx, lam, h0)


def _lru_out_kernel(hf_ref, hb_ref, g_ref, o_ref):
    o_ref[0] = ((hf_ref[0] + hb_ref[0]) * jax.nn.gelu(g_ref[0])).astype(BF16)


def _lru_out(hf, hb, proj, col_blk):
    B, N, W = hf.shape
    tm = _tile(N, 512)
    tok = pl.BlockSpec((1, tm, W), lambda b, i: (b, i, 0))
    return pl.pallas_call(
        _lru_out_kernel,
        grid=(B, N // tm),
        in_specs=[tok, tok, pl.BlockSpec((1, tm, W), lambda b, i: (b, i, col_blk))],
        out_specs=tok,
        out_shape=jax.ShapeDtypeStruct((B, N, W), BF16),
        compiler_params=_cparams("parallel", "parallel"),
        name="rglru_gate_out",
    )(hf, hb, proj)


def _rope(x, cosf, sins, perm):
    return x * cosf + jnp.dot(x.astype(BF16), perm, preferred_element_type=F32) * sins


def _mla_prep_kernel(qc_ref, kvc_ref, kr_ref, gqc_ref, gkvc_ref, wuq_ref, wukv_ref, gq_ref, gk_ref,
                     cos_ref, sin_ref, perm_ref, q_ref, k_ref, v_ref):
    cosf, sins, perm = cos_ref[...], sin_ref[...], perm_ref[...]
    gqs = gq_ref[...] * (MLA_QK ** -0.5 * LOG2E)

    qc = qc_ref[0]
    qn = (qc * lax.rsqrt(jnp.mean(qc * qc, axis=-1, keepdims=True) + EPS)) * gqc_ref[...]
    qf = jnp.dot(qn.astype(BF16), wuq_ref[...], preferred_element_type=F32)
    kvc = kvc_ref[0]
    kvn = (kvc * lax.rsqrt(jnp.mean(kvc * kvc, axis=-1, keepdims=True) + EPS)) * gkvc_ref[...]
    kvf = jnp.dot(kvn.astype(BF16), wukv_ref[...], preferred_element_type=F32)

    kr = kr_ref[0]
    ss_kr = jnp.sum(kr * kr, axis=-1, keepdims=True)
    kr_rot = _rope(kr * gk_ref[:, LANE:], cosf, sins, perm)

    for h in range(MLA_HEADS):
        o = h * MLA_QK_PAD
        qa = qf[:, o:o + LANE]
        qb = qf[:, o + LANE:o + 2 * LANE]
        ss = jnp.sum(qa * qa + qb * qb, axis=-1, keepdims=True)
        rinv = lax.rsqrt(ss * (1.0 / MLA_QK) + EPS)
        q_ref[0, h, :, 0:LANE] = ((qa * rinv) * gqs[:, 0:LANE]).astype(BF16)
        q_ref[0, h, :, LANE:] = _rope((qb * rinv) * gqs[:, LANE:], cosf, sins, perm).astype(BF16)

        kn = kvf[:, o:o + LANE]
        ssk = jnp.sum(kn * kn, axis=-1, keepdims=True) + ss_kr
        rk = lax.rsqrt(ssk * (1.0 / MLA_QK) + EPS)
        k_ref[0, h, :, 0:LANE] = ((kn * rk) * gk_ref[:, 0:LANE]).astype(BF16)
        k_ref[0, h, :, LANE:] = (kr_rot * rk).astype(BF16)
        v_ref[0, h] = kvf[:, o + LANE:o + 2 * LANE].astype(BF16)


def _mla_prep(proj, cols, gqc, gkvc, wuq, wukv, gq, gk, cosf, sins, perm):
    B, N, _ = proj.shape
    tm = _tile(N, 256)
    QR, KR = gqc.shape[1], gkvc.shape[1]
    HP = MLA_HEADS * MLA_QK_PAD
    tab = pl.BlockSpec((tm, LANE), lambda b, i: (i, 0))

    def full(a):
        return pl.BlockSpec(a.shape, lambda b, i: (0,) * a.ndim)

    hd = lambda w: pl.BlockSpec((1, MLA_HEADS, tm, w), lambda b, i: (b, 0, i, 0))
    return pl.pallas_call(
        _mla_prep_kernel,
        grid=(B, N // tm),
        in_specs=[pl.BlockSpec((1, tm, QR), lambda b, i: (b, i, cols["qc"] // QR)),
                  pl.BlockSpec((1, tm, KR), lambda b, i: (b, i, cols["kvc"] // KR)),
                  pl.BlockSpec((1, tm, LANE), lambda b, i: (b, i, cols["kr"] // LANE)),
                  full(gqc), full(gkvc), full(wuq), full(wukv), full(gq), full(gk), tab, tab, full(perm)],
        out_specs=[hd(MLA_QK_PAD), hd(MLA_QK_PAD), hd(MLA_V)],
        out_shape=[jax.ShapeDtypeStruct((B, MLA_HEADS, N, MLA_QK_PAD), BF16),
                   jax.ShapeDtypeStruct((B, MLA_HEADS, N, MLA_QK_PAD), BF16),
                   jax.ShapeDtypeStruct((B, MLA_HEADS, N, MLA_V), BF16)],
        compiler_params=_cparams("parallel", "parallel"),
        name="mla_qkv_prep",
    )(proj, proj, proj, gqc, gkvc, wuq, wukv, gq, gk, cosf, sins, perm)


def _diff_prep_kernel(dq_ref, dk_ref, dv_ref, gq_ref, gk_ref, cos_ref, sin_ref, perm_ref, seg_ref,
                      q_ref, k_ref, v_ref):
    cosf, sins, perm, seg = cos_ref[...], sin_ref[...], perm_ref[...], seg_ref[...]
    gqs = gq_ref[...] * (DIFF_D ** -0.5 * LOG2E)
    gk = gk_ref[...]

    def norm_rope(x, g):
        ss = jnp.dot((x * x).astype(BF16), seg, preferred_element_type=F32)
        return _rope((x * lax.rsqrt(ss * (1.0 / DIFF_D) + EPS)) * g, cosf, sins, perm)

    for h in range(DIFF_HEADS):
        sl = slice(h * LANE, (h + 1) * LANE)
        q_ref[0, h] = norm_rope(dq_ref[0, :, sl], gqs).astype(BF16)
        k_ref[0, h] = norm_rope(dk_ref[0, :, sl], gk).astype(BF16)
        v_ref[0, h] = dv_ref[0, :, sl].astype(BF16)


def _diff_prep(proj, cols, gq2, gk2, cosf, sins, perm, seg):
    B, N, _ = proj.shape
    tm = _tile(N, 256)
    HW = DIFF_HEADS * LANE
    tab = pl.BlockSpec((tm, LANE), lambda b, i: (i, 0))
    mat = pl.BlockSpec((LANE, LANE), lambda b, i: (0, 0))
    g = pl.BlockSpec((1, LANE), lambda b, i: (0, 0))
    hd = pl.BlockSpec((1, DIFF_HEADS, tm, LANE), lambda b, i: (b, 0, i, 0))
    shp = jax.ShapeDtypeStruct((B, DIFF_HEADS, N, LANE), BF16)
    return pl.pallas_call(
        _diff_prep_kernel,
        grid=(B, N // tm),
        in_specs=[pl.BlockSpec((1, tm, HW), lambda b, i: (b, i, cols["dq"] // HW)),
                  pl.BlockSpec((1, tm, HW), lambda b, i: (b, i, cols["dk"] // HW)),
                  pl.BlockSpec((1, tm, HW), lambda b, i: (b, i, cols["dv"] // HW)),
                  g, g, tab, tab, mat, mat],
        out_specs=[hd, hd, hd],
        out_shape=[shp, shp, shp],
        compiler_params=_cparams("parallel", "parallel"),
        name="diff_qkv_prep",
    )(proj, proj, proj, gq2, gk2, cosf, sins, perm, seg)


def _softmax_step(s, v, shift, m_scr, l_scr, acc_scr):
    tk = s.shape[1]
    if shift is not None:
        p = jnp.exp2(s - shift)
        part = p[:, 0:LANE]
        for c in range(1, tk // LANE):
            part = part + p[:, c * LANE:(c + 1) * LANE]
        l_scr[...] += part
        acc_scr[...] += jnp.dot(p.astype(BF16), v, preferred_element_type=F32)
        return
    m_prev = m_scr[...]
    m_new = jnp.maximum(m_prev, jnp.max(s, axis=1, keepdims=True))
    alpha = jnp.exp2(m_prev - m_new)
    m_b = jnp.concatenate([m_new] * (tk // LANE), axis=1) if tk % LANE == 0 else m_new[:, 0:1]
    p = jnp.exp2(s - m_b)
    l_scr[...] = alpha * l_scr[...] + jnp.sum(p, axis=1, keepdims=True)
    acc_scr[...] = alpha * acc_scr[...] + jnp.dot(p.astype(BF16), v, preferred_element_type=F32)
    m_scr[...] = m_new


def _row_sum(l_scr, bounded):
    return jnp.sum(l_scr[...], axis=1, keepdims=True) if bounded else l_scr[...]


def _qk(q, k):
    return lax.dot_general(q, k, (((1,), (1,)), ((), ())), preferred_element_type=F32)


def _kv_loop(kv_refs, tk, step, unroll):
    for c in range(len(kv_refs) // 2):
        k_ref, v_ref = kv_refs[2 * c], kv_refs[2 * c + 1]
        nk = k_ref.shape[2]
        t = _tile(nk, tk)

        def body(j, carry, k_ref=k_ref, v_ref=v_ref, t=t):
            r0 = pl.multiple_of(j * t, t)
            step(k_ref[0, 0, pl.ds(r0, t), :], v_ref[0, 0, pl.ds(r0, t), :])
            return carry

        lax.fori_loop(0, nk // t, body, 0, unroll=math.gcd(nk // t, unroll))


def _mla_attn_kernel(*refs, n_kv, tk, bounded):
    shift_ref, q_ref = refs[0], refs[1]
    kv_refs = refs[2:2 + 2 * n_kv]
    o_ref = refs[2 + 2 * n_kv]
    m_scr, l_scr, acc_scr = refs[3 + 2 * n_kv:]
    if not bounded:
        m_scr[...] = jnp.full(m_scr.shape, -jnp.inf, F32)
    l_scr[...] = jnp.zeros(l_scr.shape, F32)
    acc_scr[...] = jnp.zeros(acc_scr.shape, F32)
    q = q_ref[0, 0]
    shift = shift_ref[0] if bounded else None
    _kv_loop(kv_refs, tk, lambda k, v: _softmax_step(_qk(q, k), v, shift, m_scr, l_scr, acc_scr),
             ATTN_UNROLL if bounded else 1)
    o_ref[0] = (acc_scr[...] / _row_sum(l_scr, bounded)).astype(BF16)


def _mla_attn(q, kvs, shift, bounded):
    B, H, N, DQ = q.shape
    tq = _tile(N, 1024)
    in_specs = [pl.BlockSpec(memory_space=pltpu.SMEM),
                pl.BlockSpec((1, 1, tq, DQ), lambda b, h, i: (b, h, i, 0))]
    args = [shift, q]
    for k, v in kvs:
        nk = k.shape[2]
        in_specs += [pl.BlockSpec((1, 1, nk, DQ), lambda b, h, i: (b, h, 0, 0)),
                     pl.BlockSpec((1, 1, nk, MLA_V), lambda b, h, i: (b, h, 0, 0))]
        args += [k, v]
    stat = pltpu.VMEM((tq, LANE), F32)
    return pl.pallas_call(
        functools.partial(_mla_attn_kernel, n_kv=len(kvs), tk=ATTN_TK, bounded=bounded),
        grid=(B, H, N // tq),
        in_specs=in_specs,
        out_specs=pl.BlockSpec((1, tq, MLA_V), lambda b, h, i: (b, i, h)),
        out_shape=jax.ShapeDtypeStruct((B, N, H * MLA_V), BF16),
        scratch_shapes=[stat, stat, stat],
        compiler_params=_cparams("parallel", "parallel", "arbitrary"),
        name="mla_attention",
    )(*args)


def _diff_attn_kernel(*refs, n_kv, tk, lam_init, bounded):
    shift_ref, q_ref = refs[0], refs[1]
    kv_refs = refs[2:2 + 2 * n_kv]
    lamp_ref, gout_ref, o_ref = refs[2 + 2 * n_kv:5 + 2 * n_kv]
    m0, l0, acc0, m1, l1, acc1 = refs[5 + 2 * n_kv:]
    if not bounded:
        for m in (m0, m1):
            m[...] = jnp.full(m.shape, -jnp.inf, F32)
    for z in (l0, l1, acc0, acc1):
        z[...] = jnp.zeros(z.shape, F32)
    q = q_ref[0, 0]
    first = lax.broadcasted_iota(jnp.int32, q.shape, 1) < DIFF_D
    zero = jnp.zeros(q.shape, BF16)
    qa = jnp.where(first, q, zero)
    qb = jnp.where(first, zero, q)
    shift = shift_ref[0] if bounded else None

    def step(k, v):
        _softmax_step(_qk(qa, k), v, shift, m0, l0, acc0)
        _softmax_step(_qk(qb, k), v, shift, m1, l1, acc1)

    _kv_loop(kv_refs, tk, step, ATTN_UNROLL if bounded else 1)

    lp = lamp_ref[...]
    lam = (jnp.exp(jnp.sum(lp[0:1] * lp[1:2], keepdims=True))
           - jnp.exp(jnp.sum(lp[2:3] * lp[3:4], keepdims=True)) + lam_init)
    o = acc0[...] / _row_sum(l0, bounded) - lam * (acc1[...] / _row_sum(l1, bounded))
    on = (o * lax.rsqrt(jnp.mean(o * o, axis=-1, keepdims=True) + EPS)) * gout_ref[...]
    o_ref[0] = (on * (1.0 - lam_init)).astype(BF16)


def _diff_attn(q, kvs, lam_p, g_out, lam_init, shift, bounded):
    B, H, N, DQ = q.shape
    tq = _tile(N, 1024)
    in_specs = [pl.BlockSpec(memory_space=pltpu.SMEM),
                pl.BlockSpec((1, 1, tq, DQ), lambda b, h, i: (b, h, i, 0))]
    args = [shift, q]
    for k, v in kvs:
        nk = k.shape[2]
        kv_spec = pl.BlockSpec((1, 1, nk, LANE), lambda b, h, i: (b, h, 0, 0))
        in_specs += [kv_spec, kv_spec]
        args += [k, v]
    in_specs += [pl.BlockSpec(lam_p.shape, lambda b, h, i: (0, 0)),
                 pl.BlockSpec((1, DIFF_V), lambda b, h, i: (0, 0))]
    args += [lam_p, g_out]
    stat = pltpu.VMEM((tq, LANE), F32)
    return pl.pallas_call(
        functools.partial(_diff_attn_kernel, n_kv=len(kvs), tk=ATTN_TK, lam_init=lam_init, bounded=bounded),
        grid=(B, H, N // tq),
        in_specs=in_specs,
        out_specs=pl.BlockSpec((1, tq, DIFF_V), lambda b, h, i: (b, i, h)),
        out_shape=jax.ShapeDtypeStruct((B, N, H * DIFF_V), BF16),
        scratch_shapes=[stat] * 6,
        compiler_params=_cparams("parallel", "parallel", "arbitrary"),
        name="diff_attention",
    )(*args)


def _merge_kernel(a_ref, b_ref, c_ref, gt_ref, p_ref, o_ref, acc_scr):
    j = pl.program_id(2)

    def contrib(br_ref):
        return jax.nn.sigmoid(gt_ref[0]) * jnp.dot(br_ref[0], p_ref[0], preferred_element_type=F32)

    @pl.when(j == 0)
    def _():
        acc_scr[...] = contrib(a_ref)

    @pl.when(j == 1)
    def _():
        acc_scr[...] += contrib(b_ref)

    @pl.when(j == 2)
    def _():
        o_ref[0] = (acc_scr[...] + contrib(c_ref)).astype(BF16)


def _merge(a, b, c, proj, bproj):
    B, N, W = a.shape
    D = bproj.shape[2]
    tm = _tile(N, 512)
    br = pl.BlockSpec((1, tm, W), lambda b_, i, j: (b_, i, 0))
    return pl.pallas_call(
        _merge_kernel,
        grid=(B, N // tm, N_BRANCH),
        in_specs=[br, br, br,
                  pl.BlockSpec((1, tm, D), lambda b_, i, j: (b_, i, j)),
                  pl.BlockSpec((1, W, D), lambda b_, i, j: (j, 0, 0))],
        out_specs=pl.BlockSpec((1, tm, D), lambda b_, i, j: (b_, i, 0)),
        out_shape=jax.ShapeDtypeStruct((B, N, D), BF16),
        scratch_shapes=[pltpu.VMEM((tm, D), F32)],
        compiler_params=_cparams("parallel", "parallel", "arbitrary"),
        name="branch_merge",
    )(a, b, c, proj, bproj)


def _outproj_kernel(x_ref, y_ref, w_ref, m2_ref, o_ref):
    o_ref[0] = x_ref[0] + m2_ref[0] * jnp.dot(y_ref[0], w_ref[...], preferred_element_type=F32)


def _outproj(x, y, w, m2):
    B, N, D = x.shape
    tm = _tile(N, 512)
    tok = pl.BlockSpec((1, tm, D), lambda b, i: (b, i, 0))
    return pl.pallas_call(
        _outproj_kernel,
        grid=(B, N // tm),
        in_specs=[tok, tok, pl.BlockSpec((D, D), lambda b, i: (0, 0)),
                  pl.BlockSpec((1, 1, D), lambda b, i: (b, 0, 0))],
        out_specs=tok,
        out_shape=jax.ShapeDtypeStruct((B, N, D), F32),
        compiler_params=_cparams("parallel", "parallel"),
        name="mixer_out_proj",
    )(x, y, w, m2)


def _rope_tables(n_tokens, identity):
    half = MLA_ROPE // 2
    if identity:
        return jnp.ones((n_tokens, LANE), F32), jnp.zeros((n_tokens, LANE), F32)
    rows = n_tokens // GRID_W
    axis_dim = MLA_ROPE // 2
    inv_freq = ROPE_THETA ** (-jnp.arange(0, axis_dim, 2, dtype=F32) / axis_dim)
    nf = axis_dim // 2
    row_ang = jnp.arange(rows, dtype=F32)[:, None] * inv_freq
    col_ang = jnp.arange(GRID_W, dtype=F32)[:, None] * inv_freq
    ang = jnp.concatenate([jnp.broadcast_to(row_ang[:, None, :], (rows, GRID_W, nf)),
                           jnp.broadcast_to(col_ang[None, :, :], (rows, GRID_W, nf))], axis=-1)
    ang = ang.reshape(rows * GRID_W, axis_dim)
    cos, sin = jnp.cos(ang), jnp.sin(ang)
    assert half == axis_dim
    return jnp.concatenate([cos, cos, cos, cos], axis=1), jnp.concatenate([-sin, sin, -sin, sin], axis=1)


def _lane_matrices():
    i = jnp.arange(LANE)
    half = MLA_ROPE // 2
    partner = jnp.where(i % MLA_ROPE < half, i + half, i - half)
    perm = (i[:, None] == partner[None, :]).astype(BF16)
    seg = (i[:, None] // DIFF_D == i[None, :] // DIFF_D).astype(BF16)
    return perm, seg


def _in_proj_layout(D, W):
    order = [("gt", N_BRANCH * D), ("rx", W), ("rg", W), ("dq", DIFF_HEADS * 2 * DIFF_D),
             ("dk", DIFF_HEADS * 2 * DIFF_D), ("dv", DIFF_HEADS * DIFF_V)]
    cols, off = {}, 0
    for name, width in order:
        if name != "gt":
            off = -(-off // width) * width
        cols[name] = off
        off += width
    return cols, off


def kernel(x, c, ctx, c_ctx, w_mod, b_mod, norm_g, ffn_w_gate, ffn_w_up, ffn_w_down, w_in, lru_conv_w,
           lru_conv_b, lru_wa, lru_ba, lru_wx, lru_bx, lru_lambda, mla_qc_norm, mla_kvc_norm, mla_w_uq,
           mla_w_ukv, mla_q_norm, mla_k_norm, diff_q_norm, diff_k_norm, diff_lambda, diff_out_norm,
           branch_proj, w_out):
    B, N, D = x.shape
    NC = ctx.shape[1]
    L = w_mod.shape[0]
    W = lru_conv_w.shape[-1]
    QR, KR = mla_qc_norm.shape[-1], mla_kvc_norm.shape[-1]
    dqw = DIFF_HEADS * 2 * DIFF_D

    sizes = dict(rx=W, rg=W, qc=QR, kvc=KR, kr=MLA_ROPE, dq=dqw, dk=dqw, dv=DIFF_HEADS * DIFF_V,
                 gt=N_BRANCH * D)
    src, off = {}, 0
    for name in ("rx", "rg", "qc", "kvc", "kr", "dq", "dk", "dv", "gt"):
        src[name] = off
        off += sizes[name]
    cols, off = _in_proj_layout(D, W)
    for name, width in (("qc", QR), ("kvc", KR), ("kr", LANE)):
        off = -(-off // width) * width
        cols[name] = off
        off += width
    NP = -(-off // 2048) * 2048 if off > 2048 else off

    def pad_w_in(w):
        out = jnp.zeros((D, NP), BF16)
        for name in sizes:
            out = lax.dynamic_update_slice(out, w[:, src[name]:src[name] + sizes[name]].astype(BF16),
                                           (0, cols[name]))
        return out

    def pad_heads(w, width):
        K = w.shape[0]
        w3 = w.reshape(K, MLA_HEADS, width)
        return jnp.pad(w3, ((0, 0), (0, 0), (0, MLA_QK_PAD - width))).reshape(K, MLA_HEADS * MLA_QK_PAD)

    R = -(-(B + 1) // SUBLANE) * SUBLANE
    cpad = jnp.zeros((R, D), F32).at[:B].set(c).at[B].set(c_ctx)
    mod = _compute_mod(cpad, w_mod, b_mod).reshape(L, R, N_MOD, D)

    perm, seg = _lane_matrices()
    tabs_l = _rope_tables(N, identity=False) + (perm,)
    tabs_c = _rope_tables(NC, identity=True) + (perm,)

    xl, xc = x, ctx.reshape(1, B * NC, D)
    for l in range(L):
        last = l == L - 1
        lam_init = 0.8 - 0.6 * math.exp(-0.3 * l)
        mod_l = mod[l, :B]
        mod_c = mod[l, B:B + 1]

        def rows(m, s):
            return m[:, 3 * s:3 * s + 1], m[:, 3 * s + 1:3 * s + 2], m[:, 3 * s + 2:3 * s + 3]

        ng = norm_g[l]
        f1 = (ffn_w_gate[l, 0].astype(BF16), ffn_w_up[l, 0].astype(BF16), ffn_w_down[l, 0].astype(BF16))
        f2 = (ffn_w_gate[l, 1].astype(BF16), ffn_w_up[l, 1].astype(BF16), ffn_w_down[l, 1].astype(BF16))
        w_in_p = pad_w_in(w_in[l])
        wuq = pad_heads(mla_w_uq[l], MLA_QK).astype(BF16)
        wukv = mla_w_ukv[l].astype(BF16)
        gq = jnp.pad(mla_q_norm[l], (0, MLA_QK_PAD - MLA_QK)).reshape(1, MLA_QK_PAD)
        gk = jnp.pad(mla_k_norm[l], (0, MLA_QK_PAD - MLA_QK)).reshape(1, MLA_QK_PAD)
        gqc = mla_qc_norm[l].reshape(1, QR)
        gkvc = mla_kvc_norm[l].reshape(1, KR)
        gdq = jnp.tile(diff_q_norm[l], 2).reshape(1, LANE)
        gdk = jnp.tile(diff_k_norm[l], 2).reshape(1, LANE)
        gdo = diff_out_norm[l].reshape(1, DIFF_V)
        bproj = branch_proj[l].astype(BF16)
        wo = w_out[l].astype(BF16)
        lru = dict(cw=lru_conv_w[l], cb=lru_conv_b[l].reshape(1, W))

        bound_m = math.sqrt(MLA_QK) * jnp.max(jnp.abs(mla_q_norm[l])) * jnp.max(jnp.abs(mla_k_norm[l]))
        bound_d = math.sqrt(DIFF_D) * jnp.max(jnp.abs(diff_q_norm[l])) * jnp.max(jnp.abs(diff_k_norm[l]))

        def mla_attend(q, kvs):
            shift = (bound_m * LOG2E).reshape(1)
            return lax.cond(bound_m <= MAX_SCORE_BOUND,
                            lambda: _mla_attn(q, kvs, shift, True), lambda: _mla_attn(q, kvs, shift, False))

        def diff_attend(q, kvs):
            shift = (bound_d * LOG2E).reshape(1)
            args = (diff_lambda[l], gdo, lam_init, shift)
            return lax.cond(bound_d <= MAX_SCORE_BOUND,
                            lambda: _diff_attn(q, kvs, *args, True), lambda: _diff_attn(q, kvs, *args, False))

        m0, m1, m2 = rows(mod_l, 0)
        xl = _ffn(xl, ng[0:1], m0, m1, m2, *f1)
        m0, m1, m2 = rows(mod_c, 0)
        xc = _ffn(xc, ng[0:1], m0, m1, m2, *f1)

        m0, m1, gate_l = rows(mod_l, 1)
        proj_l = _inproj(xl, ng[1:2], m0, m1, w_in_p)
        m0, m1, gate_c = rows(mod_c, 1)
        proj_c = _inproj(xc, ng[1:2], m0, m1, w_in_p).reshape(B, NC, NP)

        h_zero = jnp.zeros((B, 1, W), F32)
        hs_l, hs_c = [], []
        for d, rev in enumerate((False, True)):
            prm = dict(cw=lru["cw"], cb=lru["cb"], wa=lru_wa[l, d].astype(BF16), ba=lru_ba[l, d].reshape(1, W),
                       wx=lru_wx[l, d].astype(BF16), bx=lru_bx[l, d].reshape(1, W),
                       lam=lru_lambda[l, d].reshape(1, W))
            hc, hc_fin = _lru_scan(proj_c, cols["rx"] // W, h_zero, reverse=rev, W=W, **prm)
            hl, _ = _lru_scan(proj_l, cols["rx"] // W, hc_fin, reverse=rev, W=W, **prm)
            hs_l.append(hl)
            hs_c.append(hc)
        a_lat = _lru_out(hs_l[0], hs_l[1], proj_l, cols["rg"] // W)

        q_ml, k_ml, v_ml = _mla_prep(proj_l, cols, gqc, gkvc, wuq, wukv, gq, gk, *tabs_l)
        q_mc, k_mc, v_mc = _mla_prep(proj_c, cols, gqc, gkvc, wuq, wukv, gq, gk, *tabs_c)
        b_lat = mla_attend(q_ml, [(k_ml, v_ml), (k_mc, v_mc)])

        q_dl, k_dl, v_dl = _diff_prep(proj_l, cols, gdq, gdk, *tabs_l, seg)
        q_dc, k_dc, v_dc = _diff_prep(proj_c, cols, gdq, gdk, *tabs_c, seg)
        c_lat = diff_attend(q_dl, [(k_dl, v_dl), (k_dc, v_dc)])

        y_lat = _merge(a_lat, b_lat, c_lat, proj_l, bproj)
        xl = _outproj(xl, y_lat, wo, gate_l)

        m0, m1, m2 = rows(mod_l, 2)
        xl = _ffn(xl, ng[2:3], m0, m1, m2, *f2)

        if not last:
            a_ctx = _lru_out(hs_c[0], hs_c[1], proj_c, cols["rg"] // W)
            b_ctx = mla_attend(q_mc, [(k_mc, v_mc)])
            c_ctx_o = diff_attend(q_dc, [(k_dc, v_dc)])
            y_ctx = _merge(a_ctx, b_ctx, c_ctx_o, proj_c, bproj)
            xc = _outproj(xc, y_ctx.reshape(1, B * NC, D), wo, gate_c)
            m0, m1, m2 = rows(mod_c, 2)
            xc = _ffn(xc, ng[2:3], m0, m1, m2, *f2)
    return xl
```

```python
import functools
import math

import jax
import jax.numpy as jnp
from jax import lax
from jax.experimental import pallas as pl
from jax.experimental.pallas import tpu as pltpu

F32 = jnp.float32
BF16 = jnp.bfloat16

EPS = 1e-6
ROPE_THETA = 10000.0
GRID_W = 64
N_BRANCH = 3
N_MOD = 9
LRU_BLOCKS = 8
CONV_W = 4
LRU_C = 8.0
MLA_HEADS = 8
MLA_NOPE = 128
MLA_ROPE = 64
MLA_V = 128
MLA_QK = MLA_NOPE + MLA_ROPE
DIFF_HEADS = 8
DIFF_D = 64
DIFF_V = 2 * DIFF_D

LANE = 128
SUBLANE = 8
MLA_QK_PAD = 256
VMEM_LIMIT = 56 * 1024 * 1024
FFN_FK = 1024
NORM_ROWS = 16
LOG2E = math.log2(math.e)
ATTN_TK = 1024
ATTN_UNROLL = 8
MAX_SCORE_BOUND = 30.0


def _cparams(*sem):
    return pltpu.CompilerParams(dimension_semantics=sem, vmem_limit_bytes=VMEM_LIMIT)


def _tile(n, pref):
    if n <= pref:
        return n
    t = pref - pref % SUBLANE
    while t > SUBLANE and n % t:
        t -= SUBLANE
    assert n % t == 0, (n, pref)
    return t


def _sigmoid(x):
    return 0.5 * jnp.tanh(0.5 * x) + 0.5


def _norm_mod_into(h_scr, x_ref, g, m0, m1):
    tm = h_scr.shape[0]
    rows = NORM_ROWS if tm % NORM_ROWS == 0 else tm
    gm = g * (1.0 + m0)

    def body(c, carry):
        r0 = pl.multiple_of(c * rows, rows)
        x = x_ref[0, pl.ds(r0, rows), :]
        ms = jnp.mean(x * x, axis=-1, keepdims=True)
        h_scr[pl.ds(r0, rows), :] = ((x * lax.rsqrt(ms + EPS)) * gm + m1).astype(BF16)
        return carry

    lax.fori_loop(0, tm // rows, body, 0, unroll=math.gcd(tm // rows, 16))


def _mod_kernel(c_ref, w_ref, b_ref, o_ref):
    c = c_ref[...]
    sc = c * jax.nn.sigmoid(c)
    o_ref[0] = jnp.dot(sc, w_ref[0], preferred_element_type=F32) + b_ref[0]


def _compute_mod(cpad, w_mod, b_mod):
    L, D, ND = w_mod.shape
    R = cpad.shape[0]
    tn = _tile(ND, 1024)
    return pl.pallas_call(
        _mod_kernel,
        grid=(L, ND // tn),
        in_specs=[pl.BlockSpec((R, D), lambda l, j: (0, 0)),
                  pl.BlockSpec((1, D, tn), lambda l, j: (l, 0, j)),
                  pl.BlockSpec((1, 1, tn), lambda l, j: (l, 0, j))],
        out_specs=pl.BlockSpec((1, R, tn), lambda l, j: (l, 0, j)),
        out_shape=jax.ShapeDtypeStruct((L, R, ND), F32),
        compiler_params=_cparams("parallel", "parallel"),
        name="adaln_mod",
    )(cpad, w_mod, b_mod.reshape(L, 1, ND))


def _ffn_kernel(x_ref, g_ref, m0_ref, m1_ref, m2_ref, wg_ref, wu_ref, wd_ref, o_ref, h_scr, *, tail):
    f = pl.program_id(2)
    last = pl.num_programs(2) - 1
    fk = wg_ref.shape[1]

    @pl.when(f == 0)
    def _():
        _norm_mod_into(h_scr, x_ref, g_ref[...], m0_ref[0], m1_ref[0])
        o_ref[0] = jnp.zeros(o_ref.shape[1:], F32)

    def chunk(width):
        h = h_scr[...]
        g = jnp.dot(h, wg_ref[:, 0:width], preferred_element_type=F32)
        u = jnp.dot(h, wu_ref[:, 0:width], preferred_element_type=F32)
        a = (g * jax.nn.sigmoid(g)) * u
        o_ref[0] += jnp.dot(a.astype(BF16), wd_ref[0:width, :], preferred_element_type=F32)

    if tail == fk:
        chunk(fk)
    else:
        pl.when(f < last)(lambda: chunk(fk))
        pl.when(f == last)(lambda: chunk(tail))

    @pl.when(f == last)
    def _():
        o_ref[0] = x_ref[0] + (0.5 * m2_ref[0]) * o_ref[0]


def _ffn(x, g, m0, m1, m2, wg, wu, wd):
    B, N, D = x.shape
    F = wg.shape[1]
    tm = _tile(N, 512)
    fk = min(F, FFN_FK)
    nf = -(-F // fk)
    tail = F - (nf - 1) * fk
    tok = pl.BlockSpec((1, tm, D), lambda b, i, f: (b, i, 0))
    row = pl.BlockSpec((1, 1, D), lambda b, i, f: (b, 0, 0))
    return pl.pallas_call(
        functools.partial(_ffn_kernel, tail=tail),
        grid=(B, N // tm, nf),
        in_specs=[tok, pl.BlockSpec((1, D), lambda b, i, f: (0, 0)), row, row, row,
                  pl.BlockSpec((D, fk), lambda b, i, f: (0, f)),
                  pl.BlockSpec((D, fk), lambda b, i, f: (0, f)),
                  pl.BlockSpec((fk, D), lambda b, i, f: (f, 0))],
        out_specs=tok,
        out_shape=jax.ShapeDtypeStruct((B, N, D), F32),
        scratch_shapes=[pltpu.VMEM((tm, D), BF16)],
        compiler_params=_cparams("parallel", "parallel", "arbitrary"),
        name="swiglu_half_step",
    )(x, g, m0, m1, m2, wg, wu, wd)


def _inproj_kernel(x_ref, g_ref, m0_ref, m1_ref, w_ref, o_ref, h_scr):
    @pl.when(pl.program_id(2) == 0)
    def _():
        _norm_mod_into(h_scr, x_ref, g_ref[...], m0_ref[0], m1_ref[0])

    o_ref[0] = jnp.dot(h_scr[...], w_ref[...], preferred_element_type=F32)


def _inproj(x, g, m0, m1, w):
    B, N, D = x.shape
    NP = w.shape[1]
    tm = _tile(N, 512)
    tn = _tile(NP, 2048)
    row = pl.BlockSpec((1, 1, D), lambda b, i, j: (b, 0, 0))
    return pl.pallas_call(
        _inproj_kernel,
        grid=(B, N // tm, NP // tn),
        in_specs=[pl.BlockSpec((1, tm, D), lambda b, i, j: (b, i, 0)),
                  pl.BlockSpec((1, D), lambda b, i, j: (0, 0)), row, row,
                  pl.BlockSpec((D, tn), lambda b, i, j: (0, j))],
        out_specs=pl.BlockSpec((1, tm, tn), lambda b, i, j: (b, i, j)),
        out_shape=jax.ShapeDtypeStruct((B, N, NP), F32),
        scratch_shapes=[pltpu.VMEM((tm, D), BF16)],
        compiler_params=_cparams("parallel", "parallel", "arbitrary"),
        name="mixer_in_proj",
    )(x, g, m0, m1, w)


def _lru_kernel(x_ref, xp_ref, xn_ref, cw_ref, cb_ref, wa_ref, ba_ref, wx_ref, bx_ref, lam_ref, h0_ref,
                h_ref, hfin_ref, xs_scr, a_scr, u_scr, carry_scr, *, reverse, nt, tm, W):
    i = pl.program_id(1)
    ti = (nt - 1 - i) if reverse else i
    H = SUBLANE

    @pl.when(i == 0)
    def _():
        carry_scr[...] = h0_ref[0]

    zero = jnp.zeros((H, W), F32)
    xs_scr[0:H, :] = jnp.where(ti > 0, xp_ref[0], zero)
    xs_scr[H:H + tm, :] = x_ref[0]
    xs_scr[H + tm:2 * H + tm, :] = jnp.where(ti < nt - 1, xn_ref[0], zero)

    sp = jax.nn.softplus(-lam_ref[...])
    lb = W // LRU_BLOCKS
    left = CONV_W // 2
    for n in range(LRU_BLOCKS):
        sl = slice(n * lb, (n + 1) * lb)
        xc = cb_ref[:, sl] + xs_scr[H - left:H - left + tm, sl] * cw_ref[0:1, sl]
        for j in range(1, CONV_W):
            xc = xc + xs_scr[H - left + j:H - left + j + tm, sl] * cw_ref[j:j + 1, sl]
        xb = xc.astype(BF16)
        r = _sigmoid(jnp.dot(xb, wa_ref[n], preferred_element_type=F32) + ba_ref[:, sl])
        ig = _sigmoid(jnp.dot(xb, wx_ref[n], preferred_element_type=F32) + bx_ref[:, sl])
        a = jnp.exp((-LRU_C) * r * sp[:, sl])
        a_scr[:, sl] = a
        u_scr[:, sl] = jnp.sqrt(1.0 - a * a) * (ig * xc)

    rowi = lax.broadcasted_iota(jnp.int32, (H, W), 0)
    ng = tm // H

    def group(gi, h):
        g = (ng - 1 - gi) if reverse else gi
        r0 = pl.multiple_of(g * H, H)
        a = a_scr[pl.ds(r0, H), :]
        u = u_scr[pl.ds(r0, H), :]
        for s in (1, 2, 4):
            sh = (H - s) if reverse else s
            valid = (rowi < H - s) if reverse else (rowi >= s)
            a_s = pltpu.roll(a, sh, 0)
            u_s = pltpu.roll(u, sh, 0)
            u = jnp.where(valid, a * u_s + u, u)
            a = jnp.where(valid, a * a_s, a)
        hh = a * h + u
        h_ref[0, pl.ds(r0, H), :] = hh
        return hh[0:1, :] if reverse else hh[H - 1:H, :]

    h_last = lax.fori_loop(0, ng, group, carry_scr[...])
    carry_scr[...] = h_last
    hfin_ref[0] = h_last


def _lru_scan(proj, col_blk, h0, cw, cb, wa, ba, wx, bx, lam, *, reverse, W):
    B, N, _ = proj.shape
    tm = _tile(N, 256)
    nt = N // tm
    r8 = tm // SUBLANE
    nb8 = N // SUBLANE

    def t_of(i):
        return (nt - 1 - i) if reverse else i

    vec = pl.BlockSpec((1, W), lambda b, i: (0, 0))
    blk = pl.BlockSpec((LRU_BLOCKS, W // LRU_BLOCKS, W // LRU_BLOCKS), lambda b, i: (0, 0, 0))
    kern = functools.partial(_lru_kernel, reverse=reverse, nt=nt, tm=tm, W=W)
    return pl.pallas_call(
        kern,
        grid=(B, nt),
        in_specs=[pl.BlockSpec((1, tm, W), lambda b, i: (b, t_of(i), col_blk)),
                  pl.BlockSpec((1, SUBLANE, W), lambda b, i: (b, jnp.maximum(t_of(i) * r8 - 1, 0), col_blk)),
                  pl.BlockSpec((1, SUBLANE, W), lambda b, i: (b, jnp.minimum((t_of(i) + 1) * r8, nb8 - 1), col_blk)),
                  pl.BlockSpec((CONV_W, W), lambda b, i: (0, 0)), vec, blk, vec, blk, vec, vec,
                  pl.BlockSpec((1, 1, W), lambda b, i: (b, 0, 0))],
        out_specs=[pl.BlockSpec((1, tm, W), lambda b, i: (b, t_of(i), 0)),
                   pl.BlockSpec((1, 1, W), lambda b, i: (b, 0, 0))],
        out_shape=[jax.ShapeDtypeStruct((B, N, W), F32), jax.ShapeDtypeStruct((B, 1, W), F32)],
        scratch_shapes=[pltpu.VMEM((tm + 2 * SUBLANE, W), F32), pltpu.VMEM((tm, W), F32),
                        pltpu.VMEM((tm, W), F32), pltpu.VMEM((1, W), F32)],
        compiler_params=_cparams("parallel", "arbitrary"),
        name="rglru_rev" if reverse else "rglru_fwd",
    )(proj, proj, proj, cw, cb, wa, ba, wx, bx, lam, h0)


def _lru_out_kernel(hf_ref, hb_ref, g_ref, o_ref):
    o_ref[0] = ((hf_ref[0] + hb_ref[0]) * jax.nn.gelu(g_ref[0])).astype(BF16)


def _lru_out(hf, hb, proj, col_blk):
    B, N, W = hf.shape
    tm = _tile(N, 512)
    tok = pl.BlockSpec((1, tm, W), lambda b, i: (b, i, 0))
    return pl.pallas_call(
        _lru_out_kernel,
        grid=(B, N // tm),
        in_specs=[tok, tok, pl.BlockSpec((1, tm, W), lambda b, i: (b, i, col_blk))],
        out_specs=tok,
        out_shape=jax.ShapeDtypeStruct((B, N, W), BF16),
        compiler_params=_cparams("parallel", "parallel"),
        name="rglru_gate_out",
    )(hf, hb, proj)


def _rope(x, cosf, sins, perm):
    return x * cosf + jnp.dot(x.astype(BF16), perm, preferred_element_type=F32) * sins


def _mla_prep_kernel(qc_ref, kvc_ref, kr_ref, gqc_ref, gkvc_ref, wuq_ref, wukv_ref, gq_ref, gk_ref,
                     cos_ref, sin_ref, perm_ref, q_ref, k_ref, v_ref):
    cosf, sins, perm = cos_ref[...], sin_ref[...], perm_ref[...]
    gqs = gq_ref[...] * (MLA_QK ** -0.5 * LOG2E)

    qc = qc_ref[0]
    qn = (qc * lax.rsqrt(jnp.mean(qc * qc, axis=-1, keepdims=True) + EPS)) * gqc_ref[...]
    qf = jnp.dot(qn.astype(BF16), wuq_ref[...], preferred_element_type=F32)
    kvc = kvc_ref[0]
    kvn = (kvc * lax.rsqrt(jnp.mean(kvc * kvc, axis=-1, keepdims=True) + EPS)) * gkvc_ref[...]
    kvf = jnp.dot(kvn.astype(BF16), wukv_ref[...], preferred_element_type=F32)

    kr = kr_ref[0]
    ss_kr = jnp.sum(kr * kr, axis=-1, keepdims=True)
    kr_rot = _rope(kr * gk_ref[:, LANE:], cosf, sins, perm)

    for h in range(MLA_HEADS):
        o = h * MLA_QK_PAD
        qa = qf[:, o:o + LANE]
        qb = qf[:, o + LANE:o + 2 * LANE]
        ss = jnp.sum(qa * qa + qb * qb, axis=-1, keepdims=True)
        rinv = lax.rsqrt(ss * (1.0 / MLA_QK) + EPS)
        q_ref[0, h, :, 0:LANE] = ((qa * rinv) * gqs[:, 0:LANE]).astype(BF16)
        q_ref[0, h, :, LANE:] = _rope((qb * rinv) * gqs[:, LANE:], cosf, sins, perm).astype(BF16)

        kn = kvf[:, o:o + LANE]
        ssk = jnp.sum(kn * kn, axis=-1, keepdims=True) + ss_kr
        rk = lax.rsqrt(ssk * (1.0 / MLA_QK) + EPS)
        k_ref[0, h, :, 0:LANE] = ((kn * rk) * gk_ref[:, 0:LANE]).astype(BF16)
        k_ref[0, h, :, LANE:] = (kr_rot * rk).astype(BF16)
        v_ref[0, h] = kvf[:, o + LANE:o + 2 * LANE].astype(BF16)


def _mla_prep(proj, cols, gqc, gkvc, wuq, wukv, gq, gk, cosf, sins, perm):
    B, N, _ = proj.shape
    tm = _tile(N, 256)
    QR, KR = gqc.shape[1], gkvc.shape[1]
    HP = MLA_HEADS * MLA_QK_PAD
    tab = pl.BlockSpec((tm, LANE), lambda b, i: (i, 0))

    def full(a):
        return pl.BlockSpec(a.shape, lambda b, i: (0,) * a.ndim)

    hd = lambda w: pl.BlockSpec((1, MLA_HEADS, tm, w), lambda b, i: (b, 0, i, 0))
    return pl.pallas_call(
        _mla_prep_kernel,
        grid=(B, N // tm),
        in_specs=[pl.BlockSpec((1, tm, QR), lambda b, i: (b, i, cols["qc"] // QR)),
                  pl.BlockSpec((1, tm, KR), lambda b, i: (b, i, cols["kvc"] // KR)),
                  pl.BlockSpec((1, tm, LANE), lambda b, i: (b, i, cols["kr"] // LANE)),
                  full(gqc), full(gkvc), full(wuq), full(wukv), full(gq), full(gk), tab, tab, full(perm)],
        out_specs=[hd(MLA_QK_PAD), hd(MLA_QK_PAD), hd(MLA_V)],
        out_shape=[jax.ShapeDtypeStruct((B, MLA_HEADS, N, MLA_QK_PAD), BF16),
                   jax.ShapeDtypeStruct((B, MLA_HEADS, N, MLA_QK_PAD), BF16),
                   jax.ShapeDtypeStruct((B, MLA_HEADS, N, MLA_V), BF16)],
        compiler_params=_cparams("parallel", "parallel"),
        name="mla_qkv_prep",
    )(proj, proj, proj, gqc, gkvc, wuq, wukv, gq, gk, cosf, sins, perm)


def _diff_prep_kernel(dq_ref, dk_ref, dv_ref, gq_ref, gk_ref, cos_ref, sin_ref, perm_ref, seg_ref,
                      q_ref, k_ref, v_ref):
    cosf, sins, perm, seg = cos_ref[...], sin_ref[...], perm_ref[...], seg_ref[...]
    gqs = gq_ref[...] * (DIFF_D ** -0.5 * LOG2E)
    gk = gk_ref[...]

    def norm_rope(x, g):
        ss = jnp.dot((x * x).astype(BF16), seg, preferred_element_type=F32)
        return _rope((x * lax.rsqrt(ss * (1.0 / DIFF_D) + EPS)) * g, cosf, sins, perm)

    for h in range(DIFF_HEADS):
        sl = slice(h * LANE, (h + 1) * LANE)
        q_ref[0, h] = norm_rope(dq_ref[0, :, sl], gqs).astype(BF16)
        k_ref[0, h] = norm_rope(dk_ref[0, :, sl], gk).astype(BF16)
        v_ref[0, h] = dv_ref[0, :, sl].astype(BF16)


def _diff_prep(proj, cols, gq2, gk2, cosf, sins, perm, seg):
    B, N, _ = proj.shape
    tm = _tile(N, 256)
    HW = DIFF_HEADS * LANE
    tab = pl.BlockSpec((tm, LANE), lambda b, i: (i, 0))
    mat = pl.BlockSpec((LANE, LANE), lambda b, i: (0, 0))
    g = pl.BlockSpec((1, LANE), lambda b, i: (0, 0))
    hd = pl.BlockSpec((1, DIFF_HEADS, tm, LANE), lambda b, i: (b, 0, i, 0))
    shp = jax.ShapeDtypeStruct((B, DIFF_HEADS, N, LANE), BF16)
    return pl.pallas_call(
        _diff_prep_kernel,
        grid=(B, N // tm),
        in_specs=[pl.BlockSpec((1, tm, HW), lambda b, i: (b, i, cols["dq"] // HW)),
                  pl.BlockSpec((1, tm, HW), lambda b, i: (b, i, cols["dk"] // HW)),
                  pl.BlockSpec((1, tm, HW), lambda b, i: (b, i, cols["dv"] // HW)),
                  g, g, tab, tab, mat, mat],
        out_specs=[hd, hd, hd],
        out_shape=[shp, shp, shp],
        compiler_params=_cparams("parallel", "parallel"),
        name="diff_qkv_prep",
    )(proj, proj, proj, gq2, gk2, cosf, sins, perm, seg)


def _softmax_step(s, v, shift, m_scr, l_scr, acc_scr):
    tk = s.shape[1]
    if shift is not None:
        p = jnp.exp2(s - shift)
        part = p[:, 0:LANE]
        for c in range(1, tk // LANE):
            part = part + p[:, c * LANE:(c + 1) * LANE]
        l_scr[...] += part
        acc_scr[...] += jnp.dot(p.astype(BF16), v, preferred_element_type=F32)
        return
    m_prev = m_scr[...]
    m_new = jnp.maximum(m_prev, jnp.max(s, axis=1, keepdims=True))
    alpha = jnp.exp2(m_prev - m_new)
    m_b = jnp.concatenate([m_new] * (tk // LANE), axis=1) if tk % LANE == 0 else m_new[:, 0:1]
    p = jnp.exp2(s - m_b)
    l_scr[...] = alpha * l_scr[...] + jnp.sum(p, axis=1, keepdims=True)
    acc_scr[...] = alpha * acc_scr[...] + jnp.dot(p.astype(BF16), v, preferred_element_type=F32)
    m_scr[...] = m_new


def _row_sum(l_scr, bounded):
    return jnp.sum(l_scr[...], axis=1, keepdims=True) if bounded else l_scr[...]


def _qk(q, k):
    return lax.dot_general(q, k, (((1,), (1,)), ((), ())), preferred_element_type=F32)


def _kv_loop(kv_refs, tk, step, unroll):
    for c in range(len(kv_refs) // 2):
        k_ref, v_ref = kv_refs[2 * c], kv_refs[2 * c + 1]
        nk = k_ref.shape[2]
        t = _tile(nk, tk)

        def body(j, carry, k_ref=k_ref, v_ref=v_ref, t=t):
            r0 = pl.multiple_of(j * t, t)
            step(k_ref[0, 0, pl.ds(r0, t), :], v_ref[0, 0, pl.ds(r0, t), :])
            return carry

        lax.fori_loop(0, nk // t, body, 0, unroll=math.gcd(nk // t, unroll))


def _mla_attn_kernel(*refs, n_kv, tk, bounded):
    shift_ref, q_ref = refs[0], refs[1]
    kv_refs = refs[2:2 + 2 * n_kv]
    o_ref = refs[2 + 2 * n_kv]
    m_scr, l_scr, acc_scr = refs[3 + 2 * n_kv:]
    if not bounded:
        m_scr[...] = jnp.full(m_scr.shape, -jnp.inf, F32)
    l_scr[...] = jnp.zeros(l_scr.shape, F32)
    acc_scr[...] = jnp.zeros(acc_scr.shape, F32)
    q = q_ref[0, 0]
    shift = shift_ref[0] if bounded else None
    _kv_loop(kv_refs, tk, lambda k, v: _softmax_step(_qk(q, k), v, shift, m_scr, l_scr, acc_scr),
             ATTN_UNROLL if bounded else 1)
    o_ref[0] = (acc_scr[...] / _row_sum(l_scr, bounded)).astype(BF16)


def _mla_attn(q, kvs, shift, bounded):
    B, H, N, DQ = q.shape
    tq = _tile(N, 1024)
    in_specs = [pl.BlockSpec(memory_space=pltpu.SMEM),
                pl.BlockSpec((1, 1, tq, DQ), lambda b, h, i: (b, h, i, 0))]
    args = [shift, q]
    for k, v in kvs:
        nk = k.shape[2]
        in_specs += [pl.BlockSpec((1, 1, nk, DQ), lambda b, h, i: (b, h, 0, 0)),
                     pl.BlockSpec((1, 1, nk, MLA_V), lambda b, h, i: (b, h, 0, 0))]
        args += [k, v]
    stat = pltpu.VMEM((tq, LANE), F32)
    return pl.pallas_call(
        functools.partial(_mla_attn_kernel, n_kv=len(kvs), tk=ATTN_TK, bounded=bounded),
        grid=(B, H, N // tq),
        in_specs=in_specs,
        out_specs=pl.BlockSpec((1, tq, MLA_V), lambda b, h, i: (b, i, h)),
        out_shape=jax.ShapeDtypeStruct((B, N, H * MLA_V), BF16),
        scratch_shapes=[stat, stat, stat],
        compiler_params=_cparams("parallel", "parallel", "arbitrary"),
        name="mla_attention",
    )(*args)


def _diff_attn_kernel(*refs, n_kv, tk, lam_init, bounded):
    shift_ref, q_ref = refs[0], refs[1]
    kv_refs = refs[2:2 + 2 * n_kv]
    lamp_ref, gout_ref, o_ref = refs[2 + 2 * n_kv:5 + 2 * n_kv]
    m0, l0, acc0, m1, l1, acc1 = refs[5 + 2 * n_kv:]
    if not bounded:
        for m in (m0, m1):
            m[...] = jnp.full(m.shape, -jnp.inf, F32)
    for z in (l0, l1, acc0, acc1):
        z[...] = jnp.zeros(z.shape, F32)
    q = q_ref[0, 0]
    first = lax.broadcasted_iota(jnp.int32, q.shape, 1) < DIFF_D
    zero = jnp.zeros(q.shape, BF16)
    qa = jnp.where(first, q, zero)
    qb = jnp.where(first, zero, q)
    shift = shift_ref[0] if bounded else None

    def step(k, v):
        _softmax_step(_qk(qa, k), v, shift, m0, l0, acc0)
        _softmax_step(_qk(qb, k), v, shift, m1, l1, acc1)

    _kv_loop(kv_refs, tk, step, ATTN_UNROLL if bounded else 1)

    lp = lamp_ref[...]
    lam = (jnp.exp(jnp.sum(lp[0:1] * lp[1:2], keepdims=True))
           - jnp.exp(jnp.sum(lp[2:3] * lp[3:4], keepdims=True)) + lam_init)
    o = acc0[...] / _row_sum(l0, bounded) - lam * (acc1[...] / _row_sum(l1, bounded))
    on = (o * lax.rsqrt(jnp.mean(o * o, axis=-1, keepdims=True) + EPS)) * gout_ref[...]
    o_ref[0] = (on * (1.0 - lam_init)).astype(BF16)


def _diff_attn(q, kvs, lam_p, g_out, lam_init, shift, bounded):
    B, H, N, DQ = q.shape
    tq = _tile(N, 1024)
    in_specs = [pl.BlockSpec(memory_space=pltpu.SMEM),
                pl.BlockSpec((1, 1, tq, DQ), lambda b, h, i: (b, h, i, 0))]
    args = [shift, q]
    for k, v in kvs:
        nk = k.shape[2]
        kv_spec = pl.BlockSpec((1, 1, nk, LANE), lambda b, h, i: (b, h, 0, 0))
        in_specs += [kv_spec, kv_spec]
        args += [k, v]
    in_specs += [pl.BlockSpec(lam_p.shape, lambda b, h, i: (0, 0)),
                 pl.BlockSpec((1, DIFF_V), lambda b, h, i: (0, 0))]
    args += [lam_p, g_out]
    stat = pltpu.VMEM((tq, LANE), F32)
    return pl.pallas_call(
        functools.partial(_diff_attn_kernel, n_kv=len(kvs), tk=ATTN_TK, lam_init=lam_init, bounded=bounded),
        grid=(B, H, N // tq),
        in_specs=in_specs,
        out_specs=pl.BlockSpec((1, tq, DIFF_V), lambda b, h, i: (b, i, h)),
        out_shape=jax.ShapeDtypeStruct((B, N, H * DIFF_V), BF16),
        scratch_shapes=[stat] * 6,
        compiler_params=_cparams("parallel", "parallel", "arbitrary"),
        name="diff_attention",
    )(*args)


def _merge_kernel(a_ref, b_ref, c_ref, gt_ref, p_ref, o_ref, acc_scr):
    j = pl.program_id(2)

    def contrib(br_ref):
        return jax.nn.sigmoid(gt_ref[0]) * jnp.dot(br_ref[0], p_ref[0], preferred_element_type=F32)

    @pl.when(j == 0)
    def _():
        acc_scr[...] = contrib(a_ref)

    @pl.when(j == 1)
    def _():
        acc_scr[...] += contrib(b_ref)

    @pl.when(j == 2)
    def _():
        o_ref[0] = (acc_scr[...] + contrib(c_ref)).astype(BF16)


def _merge(a, b, c, proj, bproj):
    B, N, W = a.shape
    D = bproj.shape[2]
    tm = _tile(N, 512)
    br = pl.BlockSpec((1, tm, W), lambda b_, i, j: (b_, i, 0))
    return pl.pallas_call(
        _merge_kernel,
        grid=(B, N // tm, N_BRANCH),
        in_specs=[br, br, br,
                  pl.BlockSpec((1, tm, D), lambda b_, i, j: (b_, i, j)),
                  pl.BlockSpec((1, W, D), lambda b_, i, j: (j, 0, 0))],
        out_specs=pl.BlockSpec((1, tm, D), lambda b_, i, j: (b_, i, 0)),
        out_shape=jax.ShapeDtypeStruct((B, N, D), BF16),
        scratch_shapes=[pltpu.VMEM((tm, D), F32)],
        compiler_params=_cparams("parallel", "parallel", "arbitrary"),
        name="branch_merge",
    )(a, b, c, proj, bproj)


def _outproj_kernel(x_ref, y_ref, w_ref, m2_ref, o_ref):
    o_ref[0] = x_ref[0] + m2_ref[0] * jnp.dot(y_ref[0], w_ref[...], preferred_element_type=F32)


def _outproj(x, y, w, m2):
    B, N, D = x.shape
    tm = _tile(N, 512)
    tok = pl.BlockSpec((1, tm, D), lambda b, i: (b, i, 0))
    return pl.pallas_call(
        _outproj_kernel,
        grid=(B, N // tm),
        in_specs=[tok, tok, pl.BlockSpec((D, D), lambda b, i: (0, 0)),
                  pl.BlockSpec((1, 1, D), lambda b, i: (b, 0, 0))],
        out_specs=tok,
        out_shape=jax.ShapeDtypeStruct((B, N, D), F32),
        compiler_params=_cparams("parallel", "parallel"),
        name="mixer_out_proj",
    )(x, y, w, m2)


def _rope_tables(n_tokens, identity):
    half = MLA_ROPE // 2
    if identity:
        return jnp.ones((n_tokens, LANE), F32), jnp.zeros((n_tokens, LANE), F32)
    rows = n_tokens // GRID_W
    axis_dim = MLA_ROPE // 2
    inv_freq = ROPE_THETA ** (-jnp.arange(0, axis_dim, 2, dtype=F32) / axis_dim)
    nf = axis_dim // 2
    row_ang = jnp.arange(rows, dtype=F32)[:, None] * inv_freq
    col_ang = jnp.arange(GRID_W, dtype=F32)[:, None] * inv_freq
    ang = jnp.concatenate([jnp.broadcast_to(row_ang[:, None, :], (rows, GRID_W, nf)),
                           jnp.broadcast_to(col_ang[None, :, :], (rows, GRID_W, nf))], axis=-1)
    ang = ang.reshape(rows * GRID_W, axis_dim)
    cos, sin = jnp.cos(ang), jnp.sin(ang)
    assert half == axis_dim
    return jnp.concatenate([cos, cos, cos, cos], axis=1), jnp.concatenate([-sin, sin, -sin, sin], axis=1)


def _lane_matrices():
    i = jnp.arange(LANE)
    half = MLA_ROPE // 2
    partner = jnp.where(i % MLA_ROPE < half, i + half, i - half)
    perm = (i[:, None] == partner[None, :]).astype(BF16)
    seg = (i[:, None] // DIFF_D == i[None, :] // DIFF_D).astype(BF16)
    return perm, seg


def _in_proj_layout(D, W):
    order = [("gt", N_BRANCH * D), ("rx", W), ("rg", W), ("dq", DIFF_HEADS * 2 * DIFF_D),
             ("dk", DIFF_HEADS * 2 * DIFF_D), ("dv", DIFF_HEADS * DIFF_V)]
    cols, off = {}, 0
    for name, width in order:
        if name != "gt":
            off = -(-off // width) * width
        cols[name] = off
        off += width
    return cols, off


def kernel(x, c, ctx, c_ctx, w_mod, b_mod, norm_g, ffn_w_gate, ffn_w_up, ffn_w_down, w_in, lru_conv_w,
           lru_conv_b, lru_wa, lru_ba, lru_wx, lru_bx, lru_lambda, mla_qc_norm, mla_kvc_norm, mla_w_uq,
           mla_w_ukv, mla_q_norm, mla_k_norm, diff_q_norm, diff_k_norm, diff_lambda, diff_out_norm,
           branch_proj, w_out):
    B, N, D = x.shape
    NC = ctx.shape[1]
    L = w_mod.shape[0]
    W = lru_conv_w.shape[-1]
    QR, KR = mla_qc_norm.shape[-1], mla_kvc_norm.shape[-1]
    dqw = DIFF_HEADS * 2 * DIFF_D

    sizes = dict(rx=W, rg=W, qc=QR, kvc=KR, kr=MLA_ROPE, dq=dqw, dk=dqw, dv=DIFF_HEADS * DIFF_V,
                 gt=N_BRANCH * D)
    src, off = {}, 0
    for name in ("rx", "rg", "qc", "kvc", "kr", "dq", "dk", "dv", "gt"):
        src[name] = off
        off += sizes[name]
    cols, off = _in_proj_layout(D, W)
    for name, width in (("qc", QR), ("kvc", KR), ("kr", LANE)):
        off = -(-off // width) * width
        cols[name] = off
        off += width
    NP = -(-off // 2048) * 2048 if off > 2048 else off

    def pad_w_in(w):
        out = jnp.zeros((D, NP), BF16)
        for name in sizes:
            out = lax.dynamic_update_slice(out, w[:, src[name]:src[name] + sizes[name]].astype(BF16),
                                           (0, cols[name]))
        return out

    def pad_heads(w, width):
        K = w.shape[0]
        w3 = w.reshape(K, MLA_HEADS, width)
        return jnp.pad(w3, ((0, 0), (0, 0), (0, MLA_QK_PAD - width))).reshape(K, MLA_HEADS * MLA_QK_PAD)

    R = -(-(B + 1) // SUBLANE) * SUBLANE
    cpad = jnp.zeros((R, D), F32).at[:B].set(c).at[B].set(c_ctx)
    mod = _compute_mod(cpad, w_mod, b_mod).reshape(L, R, N_MOD, D)

    perm, seg = _lane_matrices()
    tabs_l = _rope_tables(N, identity=False) + (perm,)
    tabs_c = _rope_tables(NC, identity=True) + (perm,)

    xl, xc = x, ctx.reshape(1, B * NC, D)
    for l in range(L):
        last = l == L - 1
        lam_init = 0.8 - 0.6 * math.exp(-0.3 * l)
        mod_l = mod[l, :B]
        mod_c = mod[l, B:B + 1]

        def rows(m, s):
            return m[:, 3 * s:3 * s + 1], m[:, 3 * s + 1:3 * s + 2], m[:, 3 * s + 2:3 * s + 3]

        ng = norm_g[l]
        f1 = (ffn_w_gate[l, 0].astype(BF16), ffn_w_up[l, 0].astype(BF16), ffn_w_down[l, 0].astype(BF16))
        f2 = (ffn_w_gate[l, 1].astype(BF16), ffn_w_up[l, 1].astype(BF16), ffn_w_down[l, 1].astype(BF16))
        w_in_p = pad_w_in(w_in[l])
        wuq = pad_heads(mla_w_uq[l], MLA_QK).astype(BF16)
        wukv = mla_w_ukv[l].astype(BF16)
        gq = jnp.pad(mla_q_norm[l], (0, MLA_QK_PAD - MLA_QK)).reshape(1, MLA_QK_PAD)
        gk = jnp.pad(mla_k_norm[l], (0, MLA_QK_PAD - MLA_QK)).reshape(1, MLA_QK_PAD)
        gqc = mla_qc_norm[l].reshape(1, QR)
        gkvc = mla_kvc_norm[l].reshape(1, KR)
        gdq = jnp.tile(diff_q_norm[l], 2).reshape(1, LANE)
        gdk = jnp.tile(diff_k_norm[l], 2).reshape(1, LANE)
        gdo = diff_out_norm[l].reshape(1, DIFF_V)
        bproj = branch_proj[l].astype(BF16)
        wo = w_out[l].astype(BF16)
        lru = dict(cw=lru_conv_w[l], cb=lru_conv_b[l].reshape(1, W))

        bound_m = math.sqrt(MLA_QK) * jnp.max(jnp.abs(mla_q_norm[l])) * jnp.max(jnp.abs(mla_k_norm[l]))
        bound_d = math.sqrt(DIFF_D) * jnp.max(jnp.abs(diff_q_norm[l])) * jnp.max(jnp.abs(diff_k_norm[l]))

        def mla_attend(q, kvs):
            shift = (bound_m * LOG2E).reshape(1)
            return lax.cond(bound_m <= MAX_SCORE_BOUND,
                            lambda: _mla_attn(q, kvs, shift, True), lambda: _mla_attn(q, kvs, shift, False))

        def diff_attend(q, kvs):
            shift = (bound_d * LOG2E).reshape(1)
            args = (diff_lambda[l], gdo, lam_init, shift)
            return lax.cond(bound_d <= MAX_SCORE_BOUND,
                            lambda: _diff_attn(q, kvs, *args, True), lambda: _diff_attn(q, kvs, *args, False))

        m0, m1, m2 = rows(mod_l, 0)
        xl = _ffn(xl, ng[0:1], m0, m1, m2, *f1)
        m0, m1, m2 = rows(mod_c, 0)
        xc = _ffn(xc, ng[0:1], m0, m1, m2, *f1)

        m0, m1, gate_l = rows(mod_l, 1)
        proj_l = _inproj(xl, ng[1:2], m0, m1, w_in_p)
        m0, m1, gate_c = rows(mod_c, 1)
        proj_c = _inproj(xc, ng[1:2], m0, m1, w_in_p).reshape(B, NC, NP)

        h_zero = jnp.zeros((B, 1, W), F32)
        hs_l, hs_c = [], []
        for d, rev in enumerate((False, True)):
            prm = dict(cw=lru["cw"], cb=lru["cb"], wa=lru_wa[l, d].astype(BF16), ba=lru_ba[l, d].reshape(1, W),
                       wx=lru_wx[l, d].astype(BF16), bx=lru_bx[l, d].reshape(1, W),
                       lam=lru_lambda[l, d].reshape(1, W))
            hc, hc_fin = _lru_scan(proj_c, cols["rx"] // W, h_zero, reverse=rev, W=W, **prm)
            hl, _ = _lru_scan(proj_l, cols["rx"] // W, hc_fin, reverse=rev, W=W, **prm)
            hs_l.append(hl)
            hs_c.append(hc)
        a_lat = _lru_out(hs_l[0], hs_l[1], proj_l, cols["rg"] // W)

        q_ml, k_ml, v_ml = _mla_prep(proj_l, cols, gqc, gkvc, wuq, wukv, gq, gk, *tabs_l)
        q_mc, k_mc, v_mc = _mla_prep(proj_c, cols, gqc, gkvc, wuq, wukv, gq, gk, *tabs_c)
        b_lat = mla_attend(q_ml, [(k_ml, v_ml), (k_mc, v_mc)])

        q_dl, k_dl, v_dl = _diff_prep(proj_l, cols, gdq, gdk, *tabs_l, seg)
        q_dc, k_dc, v_dc = _diff_prep(proj_c, cols, gdq, gdk, *tabs_c, seg)
        c_lat = diff_attend(q_dl, [(k_dl, v_dl), (k_dc, v_dc)])

        y_lat = _merge(a_lat, b_lat, c_lat, proj_l, bproj)
        xl = _outproj(xl, y_lat, wo, gate_l)

        m0, m1, m2 = rows(mod_l, 2)
        xl = _ffn(xl, ng[2:3], m0, m1, m2, *f2)

        if not last:
            a_ctx = _lru_out(hs_c[0], hs_c[1], proj_c, cols["rg"] // W)
            b_ctx = mla_attend(q_mc, [(k_mc, v_mc)])
            c_ctx_o = diff_attend(q_dc, [(k_dc, v_dc)])
            y_ctx = _merge(a_ctx, b_ctx, c_ctx_o, proj_c, bproj)
            xc = _outproj(xc, y_ctx.reshape(1, B * NC, D), wo, gate_c)
            m0, m1, m2 = rows(mod_c, 2)
            xc = _ffn(xc, ng[2:3], m0, m1, m2, *f2)
    return xl
```

```python
import functools
import math

import jax
import jax.numpy as jnp
from jax import lax
from jax.experimental import pallas as pl
from jax.experimental.pallas import tpu as pltpu

F32 = jnp.float32
BF16 = jnp.bfloat16

EPS = 1e-6
ROPE_THETA = 10000.0
GRID_W = 64
N_BRANCH = 3
N_MOD = 9
LRU_BLOCKS = 8
CONV_W = 4
LRU_C = 8.0
MLA_HEADS = 8
MLA_NOPE = 128
MLA_ROPE = 64
MLA_V = 128
MLA_QK = MLA_NOPE + MLA_ROPE
DIFF_HEADS = 8
DIFF_D = 64
DIFF_V = 2 * DIFF_D

LANE = 128
SUBLANE = 8
MLA_QK_PAD = 256
VMEM_LIMIT = 56 * 1024 * 1024
FFN_FK = 512
NORM_ROWS = 16
LOG2E = math.log2(math.e)
ATTN_TK = 1024
ATTN_UNROLL = 8
MAX_SCORE_BOUND = 30.0


def _cparams(*sem):
    return pltpu.CompilerParams(dimension_semantics=sem, vmem_limit_bytes=VMEM_LIMIT)


def _tile(n, pref):
    if n <= pref:
        return n
    t = pref - pref % SUBLANE
    while t > SUBLANE and n % t:
        t -= SUBLANE
    assert n % t == 0, (n, pref)
    return t


def _sigmoid(x):
    return 0.5 * jnp.tanh(0.5 * x) + 0.5


def _norm_mod_into(h_scr, x_ref, g, m0, m1):
    tm = h_scr.shape[0]
    rows = NORM_ROWS if tm % NORM_ROWS == 0 else tm
    gm = g * (1.0 + m0)

    def body(c, carry):
        r0 = pl.multiple_of(c * rows, rows)
        x = x_ref[0, pl.ds(r0, rows), :]
        ms = jnp.mean(x * x, axis=-1, keepdims=True)
        h_scr[pl.ds(r0, rows), :] = ((x * lax.rsqrt(ms + EPS)) * gm + m1).astype(BF16)
        return carry

    lax.fori_loop(0, tm // rows, body, 0, unroll=math.gcd(tm // rows, 16))


def _mod_kernel(c_ref, w_ref, b_ref, o_ref):
    c = c_ref[...]
    sc = c * jax.nn.sigmoid(c)
    o_ref[0] = jnp.dot(sc, w_ref[0], preferred_element_type=F32) + b_ref[0]


def _compute_mod(cpad, w_mod, b_mod):
    L, D, ND = w_mod.shape
    R = cpad.shape[0]
    tn = _tile(ND, 1024)
    return pl.pallas_call(
        _mod_kernel,
        grid=(L, ND // tn),
        in_specs=[pl.BlockSpec((R, D), lambda l, j: (0, 0)),
                  pl.BlockSpec((1, D, tn), lambda l, j: (l, 0, j)),
                  pl.BlockSpec((1, 1, tn), lambda l, j: (l, 0, j))],
        out_specs=pl.BlockSpec((1, R, tn), lambda l, j: (l, 0, j)),
        out_shape=jax.ShapeDtypeStruct((L, R, ND), F32),
        compiler_params=_cparams("parallel", "parallel"),
        name="adaln_mod",
    )(cpad, w_mod, b_mod.reshape(L, 1, ND))


def _ffn_kernel(x_ref, g_ref, m0_ref, m1_ref, m2_ref, wg_ref, wu_ref, wd_ref, o_ref, h_scr, *, tail):
    f = pl.program_id(2)
    last = pl.num_programs(2) - 1
    fk = wg_ref.shape[1]

    @pl.when(f == 0)
    def _():
        _norm_mod_into(h_scr, x_ref, g_ref[...], m0_ref[0], m1_ref[0])
        o_ref[0] = jnp.zeros(o_ref.shape[1:], F32)

    def chunk(width):
        h = h_scr[...]
        g = jnp.dot(h, wg_ref[:, 0:width], preferred_element_type=F32)
        u = jnp.dot(h, wu_ref[:, 0:width], preferred_element_type=F32)
        a = (g * jax.nn.sigmoid(g)) * u
        o_ref[0] += jnp.dot(a.astype(BF16), wd_ref[0:width, :], preferred_element_type=F32)

    if tail == fk:
        chunk(fk)
    else:
        pl.when(f < last)(lambda: chunk(fk))
        pl.when(f == last)(lambda: chunk(tail))

    @pl.when(f == last)
    def _():
        o_ref[0] = x_ref[0] + (0.5 * m2_ref[0]) * o_ref[0]


def _ffn(x, g, m0, m1, m2, wg, wu, wd):
    B, N, D = x.shape
    F = wg.shape[1]
    tm = _tile(N, 512)
    fk = min(F, FFN_FK)
    nf = -(-F // fk)
    tail = F - (nf - 1) * fk
    tok = pl.BlockSpec((1, tm, D), lambda b, i, f: (b, i, 0))
    row = pl.BlockSpec((1, 1, D), lambda b, i, f: (b, 0, 0))
    return pl.pallas_call(
        functools.partial(_ffn_kernel, tail=tail),
        grid=(B, N // tm, nf),
        in_specs=[tok, pl.BlockSpec((1, D), lambda b, i, f: (0, 0)), row, row, row,
                  pl.BlockSpec((D, fk), lambda b, i, f: (0, f)),
                  pl.BlockSpec((D, fk), lambda b, i, f: (0, f)),
                  pl.BlockSpec((fk, D), lambda b, i, f: (f, 0))],
        out_specs=tok,
        out_shape=jax.ShapeDtypeStruct((B, N, D), F32),
        scratch_shapes=[pltpu.VMEM((tm, D), BF16)],
        compiler_params=_cparams("parallel", "parallel", "arbitrary"),
        name="swiglu_half_step",
    )(x, g, m0, m1, m2, wg, wu, wd)


def _inproj_kernel(x_ref, g_ref, m0_ref, m1_ref, w_ref, o_ref, h_scr):
    @pl.when(pl.program_id(2) == 0)
    def _():
        _norm_mod_into(h_scr, x_ref, g_ref[...], m0_ref[0], m1_ref[0])

    o_ref[0] = jnp.dot(h_scr[...], w_ref[...], preferred_element_type=F32)


def _inproj(x, g, m0, m1, w):
    B, N, D = x.shape
    NP = w.shape[1]
    tm = _tile(N, 512)
    tn = _tile(NP, 2048)
    row = pl.BlockSpec((1, 1, D), lambda b, i, j: (b, 0, 0))
    return pl.pallas_call(
        _inproj_kernel,
        grid=(B, N // tm, NP // tn),
        in_specs=[pl.BlockSpec((1, tm, D), lambda b, i, j: (b, i, 0)),
                  pl.BlockSpec((1, D), lambda b, i, j: (0, 0)), row, row,
                  pl.BlockSpec((D, tn), lambda b, i, j: (0, j))],
        out_specs=pl.BlockSpec((1, tm, tn), lambda b, i, j: (b, i, j)),
        out_shape=jax.ShapeDtypeStruct((B, N, NP), F32),
        scratch_shapes=[pltpu.VMEM((tm, D), BF16)],
        compiler_params=_cparams("parallel", "parallel", "arbitrary"),
        name="mixer_in_proj",
    )(x, g, m0, m1, w)


def _lru_gates_and_scan(xc_of, wa_ref, ba_ref, wx_ref, bx_ref, lam_ref, h0_ref, store_h, hfin_ref,
                        a_scr, u_scr, carry_scr, *, reverse, tm, W):
    H = SUBLANE

    @pl.when(pl.program_id(1) == 0)
    def _():
        carry_scr[...] = h0_ref[0]

    sp = jax.nn.softplus(-lam_ref[...])
    lb = W // LRU_BLOCKS
    for n in range(LRU_BLOCKS):
        sl = slice(n * lb, (n + 1) * lb)
        xc = xc_of(sl)
        xb = xc.astype(BF16)
        r = _sigmoid(jnp.dot(xb, wa_ref[n], preferred_element_type=F32) + ba_ref[:, sl])
        ig = _sigmoid(jnp.dot(xb, wx_ref[n], preferred_element_type=F32) + bx_ref[:, sl])
        a = jnp.exp((-LRU_C) * r * sp[:, sl])
        a_scr[:, sl] = a
        u_scr[:, sl] = jnp.sqrt(1.0 - a * a) * (ig * xc)

    rowi = lax.broadcasted_iota(jnp.int32, (H, W), 0)
    ng = tm // H

    def group(gi, h):
        g = (ng - 1 - gi) if reverse else gi
        r0 = pl.multiple_of(g * H, H)
        a = a_scr[pl.ds(r0, H), :]
        u = u_scr[pl.ds(r0, H), :]
        for s in (1, 2, 4):
            sh = (H - s) if reverse else s
            valid = (rowi < H - s) if reverse else (rowi >= s)
            a_s = pltpu.roll(a, sh, 0)
            u_s = pltpu.roll(u, sh, 0)
            u = jnp.where(valid, a * u_s + u, u)
            a = jnp.where(valid, a * a_s, a)
        hh = a * h + u
        store_h(r0, hh)
        return hh[0:1, :] if reverse else hh[H - 1:H, :]

    h_last = lax.fori_loop(0, ng, group, carry_scr[...])
    carry_scr[...] = h_last
    hfin_ref[0] = h_last


def _lru_fwd_kernel(x_ref, xp_ref, xn_ref, cw_ref, cb_ref, wa_ref, ba_ref, wx_ref, bx_ref, lam_ref, h0_ref,
                    h_ref, xc_ref, hfin_ref, xs_scr, a_scr, u_scr, carry_scr, *, nt, tm, W):
    ti = pl.program_id(1)
    H = SUBLANE
    zero = jnp.zeros((H, W), F32)
    xs_scr[0:H, :] = jnp.where(ti > 0, xp_ref[0], zero)
    xs_scr[H:H + tm, :] = x_ref[0]
    xs_scr[H + tm:2 * H + tm, :] = jnp.where(ti < nt - 1, xn_ref[0], zero)
    left = CONV_W // 2

    def conv(sl):
        xc = cb_ref[:, sl] + xs_scr[H - left:H - left + tm, sl] * cw_ref[0:1, sl]
        for j in range(1, CONV_W):
            xc = xc + xs_scr[H - left + j:H - left + j + tm, sl] * cw_ref[j:j + 1, sl]
        xc_ref[0, :, sl] = xc
        return xc

    def store_h(r0, hh):
        h_ref[0, pl.ds(r0, H), :] = hh

    _lru_gates_and_scan(conv, wa_ref, ba_ref, wx_ref, bx_ref, lam_ref, h0_ref, store_h, hfin_ref,
                        a_scr, u_scr, carry_scr, reverse=False, tm=tm, W=W)


def _lru_rev_kernel(xc_ref, hf_ref, rg_ref, wa_ref, ba_ref, wx_ref, bx_ref, lam_ref, h0_ref,
                    y_ref, hfin_ref, hb_scr, a_scr, u_scr, carry_scr, *, tm, W):
    def store_h(r0, hh):
        hb_scr[pl.ds(r0, SUBLANE), :] = hh

    _lru_gates_and_scan(lambda sl: xc_ref[0, :, sl], wa_ref, ba_ref, wx_ref, bx_ref, lam_ref, h0_ref,
                        store_h, hfin_ref, a_scr, u_scr, carry_scr, reverse=True, tm=tm, W=W)
    lb = W // LRU_BLOCKS
    for n in range(LRU_BLOCKS):
        sl = slice(n * lb, (n + 1) * lb)
        y_ref[0, :, sl] = ((hf_ref[0, :, sl] + hb_scr[:, sl]) * jax.nn.gelu(rg_ref[0, :, sl])).astype(BF16)


def _lru_specs(W):
    vec = pl.BlockSpec((1, W), lambda b, i: (0, 0))
    blk = pl.BlockSpec((LRU_BLOCKS, W // LRU_BLOCKS, W // LRU_BLOCKS), lambda b, i: (0, 0, 0))
    state = pl.BlockSpec((1, 1, W), lambda b, i: (b, 0, 0))
    return vec, blk, state


def _lru_fwd(proj, col_blk, h0, cw, cb, wa, ba, wx, bx, lam, *, W):
    B, N, _ = proj.shape
    tm = _tile(N, 256)
    nt = N // tm
    r8 = tm // SUBLANE
    nb8 = N // SUBLANE
    vec, blk, state = _lru_specs(W)
    tok = pl.BlockSpec((1, tm, W), lambda b, i: (b, i, 0))
    full = jax.ShapeDtypeStruct((B, N, W), F32)
    return pl.pallas_call(
        functools.partial(_lru_fwd_kernel, nt=nt, tm=tm, W=W),
        grid=(B, nt),
        in_specs=[pl.BlockSpec((1, tm, W), lambda b, i: (b, i, col_blk)),
                  pl.BlockSpec((1, SUBLANE, W), lambda b, i: (b, jnp.maximum(i * r8 - 1, 0), col_blk)),
                  pl.BlockSpec((1, SUBLANE, W), lambda b, i: (b, jnp.minimum((i + 1) * r8, nb8 - 1), col_blk)),
                  pl.BlockSpec((CONV_W, W), lambda b, i: (0, 0)), vec, blk, vec, blk, vec, vec, state],
        out_specs=[tok, tok, state],
        out_shape=[full, full, jax.ShapeDtypeStruct((B, 1, W), F32)],
        scratch_shapes=[pltpu.VMEM((tm + 2 * SUBLANE, W), F32), pltpu.VMEM((tm, W), F32),
                        pltpu.VMEM((tm, W), F32), pltpu.VMEM((1, W), F32)],
        compiler_params=_cparams("parallel", "arbitrary"),
        name="rglru_fwd",
    )(proj, proj, proj, cw, cb, wa, ba, wx, bx, lam, h0)


def _lru_rev(xc, hf, proj, gate_blk, h0, wa, ba, wx, bx, lam, *, W):
    B, N, _ = xc.shape
    tm = _tile(N, 256)
    nt = N // tm
    vec, blk, state = _lru_specs(W)
    tok = pl.BlockSpec((1, tm, W), lambda b, i: (b, nt - 1 - i, 0))
    return pl.pallas_call(
        functools.partial(_lru_rev_kernel, tm=tm, W=W),
        grid=(B, nt),
        in_specs=[tok, tok, pl.BlockSpec((1, tm, W), lambda b, i: (b, nt - 1 - i, gate_blk)),
                  blk, vec, blk, vec, vec, state],
        out_specs=[tok, state],
        out_shape=[jax.ShapeDtypeStruct((B, N, W), BF16), jax.ShapeDtypeStruct((B, 1, W), F32)],
        scratch_shapes=[pltpu.VMEM((tm, W), F32), pltpu.VMEM((tm, W), F32),
                        pltpu.VMEM((tm, W), F32), pltpu.VMEM((1, W), F32)],
        compiler_params=_cparams("parallel", "arbitrary"),
        name="rglru_rev_gate",
    )(xc, hf, proj, wa, ba, wx, bx, lam, h0)


def _rope(x, cosf, sins, perm):
    return x * cosf + jnp.dot(x.astype(BF16), perm, preferred_element_type=F32) * sins


def _mla_prep_kernel(qc_ref, kvc_ref, kr_ref, gqc_ref, gkvc_ref, wuq_ref, wukv_ref, gq_ref, gk_ref,
                     cos_ref, sin_ref, perm_ref, q_ref, k_ref, v_ref):
    cosf, sins, perm = cos_ref[...], sin_ref[...], perm_ref[...]
    gqs = gq_ref[...] * (MLA_QK ** -0.5 * LOG2E)

    qc = qc_ref[0]
    qn = (qc * lax.rsqrt(jnp.mean(qc * qc, axis=-1, keepdims=True) + EPS)) * gqc_ref[...]
    qf = jnp.dot(qn.astype(BF16), wuq_ref[...], preferred_element_type=F32)
    kvc = kvc_ref[0]
    kvn = (kvc * lax.rsqrt(jnp.mean(kvc * kvc, axis=-1, keepdims=True) + EPS)) * gkvc_ref[...]
    kvf = jnp.dot(kvn.astype(BF16), wukv_ref[...], preferred_element_type=F32)

    kr = kr_ref[0]
    ss_kr = jnp.sum(kr * kr, axis=-1, keepdims=True)
    kr_rot = _rope(kr * gk_ref[:, LANE:], cosf, sins, perm)

    for h in range(MLA_HEADS):
        o = h * MLA_QK_PAD
        qa = qf[:, o:o + LANE]
        qb = qf[:, o + LANE:o + 2 * LANE]
        ss = jnp.sum(qa * qa + qb * qb, axis=-1, keepdims=True)
        rinv = lax.rsqrt(ss * (1.0 / MLA_QK) + EPS)
        q_ref[0, h, :, 0:LANE] = ((qa * rinv) * gqs[:, 0:LANE]).astype(BF16)
        q_ref[0, h, :, LANE:] = _rope((qb * rinv) * gqs[:, LANE:], cosf, sins, perm).astype(BF16)

        kn = kvf[:, o:o + LANE]
        ssk = jnp.sum(kn * kn, axis=-1, keepdims=True) + ss_kr
        rk = lax.rsqrt(ssk * (1.0 / MLA_QK) + EPS)
        k_ref[0, h, :, 0:LANE] = ((kn * rk) * gk_ref[:, 0:LANE]).astype(BF16)
        k_ref[0, h, :, LANE:] = (kr_rot * rk).astype(BF16)
        v_ref[0, h] = kvf[:, o + LANE:o + 2 * LANE].astype(BF16)


def _mla_prep(proj, cols, gqc, gkvc, wuq, wukv, gq, gk, cosf, sins, perm):
    B, N, _ = proj.shape
    tm = _tile(N, 256)
    QR, KR = gqc.shape[1], gkvc.shape[1]
    HP = MLA_HEADS * MLA_QK_PAD
    tab = pl.BlockSpec((tm, LANE), lambda b, i: (i, 0))

    def full(a):
        return pl.BlockSpec(a.shape, lambda b, i: (0,) * a.ndim)

    hd = lambda w: pl.BlockSpec((1, MLA_HEADS, tm, w), lambda b, i: (b, 0, i, 0))
    return pl.pallas_call(
        _mla_prep_kernel,
        grid=(B, N // tm),
        in_specs=[pl.BlockSpec((1, tm, QR), lambda b, i: (b, i, cols["qc"] // QR)),
                  pl.BlockSpec((1, tm, KR), lambda b, i: (b, i, cols["kvc"] // KR)),
                  pl.BlockSpec((1, tm, LANE), lambda b, i: (b, i, cols["kr"] // LANE)),
                  full(gqc), full(gkvc), full(wuq), full(wukv), full(gq), full(gk), tab, tab, full(perm)],
        out_specs=[hd(MLA_QK_PAD), hd(MLA_QK_PAD), hd(MLA_V)],
        out_shape=[jax.ShapeDtypeStruct((B, MLA_HEADS, N, MLA_QK_PAD), BF16),
                   jax.ShapeDtypeStruct((B, MLA_HEADS, N, MLA_QK_PAD), BF16),
                   jax.ShapeDtypeStruct((B, MLA_HEADS, N, MLA_V), BF16)],
        compiler_params=_cparams("parallel", "parallel"),
        name="mla_qkv_prep",
    )(proj, proj, proj, gqc, gkvc, wuq, wukv, gq, gk, cosf, sins, perm)


def _diff_prep_kernel(dq_ref, dk_ref, dv_ref, gq_ref, gk_ref, cos_ref, sin_ref, perm_ref, seg_ref,
                      q_ref, k_ref, v_ref):
    cosf, sins, perm, seg = cos_ref[...], sin_ref[...], perm_ref[...], seg_ref[...]
    gqs = gq_ref[...] * (DIFF_D ** -0.5 * LOG2E)
    gk = gk_ref[...]

    def norm_rope(x, g):
        ss = jnp.dot((x * x).astype(BF16), seg, preferred_element_type=F32)
        return _rope((x * lax.rsqrt(ss * (1.0 / DIFF_D) + EPS)) * g, cosf, sins, perm)

    for h in range(DIFF_HEADS):
        sl = slice(h * LANE, (h + 1) * LANE)
        q_ref[0, h] = norm_rope(dq_ref[0, :, sl], gqs).astype(BF16)
        k_ref[0, h] = norm_rope(dk_ref[0, :, sl], gk).astype(BF16)
        v_ref[0, h] = dv_ref[0, :, sl].astype(BF16)


def _diff_prep(proj, cols, gq2, gk2, cosf, sins, perm, seg):
    B, N, _ = proj.shape
    tm = _tile(N, 256)
    HW = DIFF_HEADS * LANE
    tab = pl.BlockSpec((tm, LANE), lambda b, i: (i, 0))
    mat = pl.BlockSpec((LANE, LANE), lambda b, i: (0, 0))
    g = pl.BlockSpec((1, LANE), lambda b, i: (0, 0))
    hd = pl.BlockSpec((1, DIFF_HEADS, tm, LANE), lambda b, i: (b, 0, i, 0))
    shp = jax.ShapeDtypeStruct((B, DIFF_HEADS, N, LANE), BF16)
    return pl.pallas_call(
        _diff_prep_kernel,
        grid=(B, N // tm),
        in_specs=[pl.BlockSpec((1, tm, HW), lambda b, i: (b, i, cols["dq"] // HW)),
                  pl.BlockSpec((1, tm, HW), lambda b, i: (b, i, cols["dk"] // HW)),
                  pl.BlockSpec((1, tm, HW), lambda b, i: (b, i, cols["dv"] // HW)),
                  g, g, tab, tab, mat, mat],
        out_specs=[hd, hd, hd],
        out_shape=[shp, shp, shp],
        compiler_params=_cparams("parallel", "parallel"),
        name="diff_qkv_prep",
    )(proj, proj, proj, gq2, gk2, cosf, sins, perm, seg)


def _softmax_step(s, v, shift, m_scr, l_scr, acc_scr):
    tk = s.shape[1]
    if shift is not None:
        p = jnp.exp2(s - shift)
        part = p[:, 0:LANE]
        for c in range(1, tk // LANE):
            part = part + p[:, c * LANE:(c + 1) * LANE]
        l_scr[...] += part
        acc_scr[...] += jnp.dot(p.astype(BF16), v, preferred_element_type=F32)
        return
    m_prev = m_scr[...]
    m_new = jnp.maximum(m_prev, jnp.max(s, axis=1, keepdims=True))
    alpha = jnp.exp2(m_prev - m_new)
    m_b = jnp.concatenate([m_new] * (tk // LANE), axis=1) if tk % LANE == 0 else m_new[:, 0:1]
    p = jnp.exp2(s - m_b)
    l_scr[...] = alpha * l_scr[...] + jnp.sum(p, axis=1, keepdims=True)
    acc_scr[...] = alpha * acc_scr[...] + jnp.dot(p.astype(BF16), v, preferred_element_type=F32)
    m_scr[...] = m_new


def _row_sum(l_scr, bounded):
    return jnp.sum(l_scr[...], axis=1, keepdims=True) if bounded else l_scr[...]


def _qk(q, k):
    return lax.dot_general(q, k, (((1,), (1,)), ((), ())), preferred_element_type=F32)


def _kv_loop(kv_refs, tk, step, unroll):
    for c in range(len(kv_refs) // 2):
        k_ref, v_ref = kv_refs[2 * c], kv_refs[2 * c + 1]
        nk = k_ref.shape[2]
        t = _tile(nk, tk)

        def body(j, carry, k_ref=k_ref, v_ref=v_ref, t=t):
            r0 = pl.multiple_of(j * t, t)
            step(k_ref[0, 0, pl.ds(r0, t), :], v_ref[0, 0, pl.ds(r0, t), :])
            return carry

        lax.fori_loop(0, nk // t, body, 0, unroll=math.gcd(nk // t, unroll))


def _mla_attn_kernel(*refs, n_kv, tk, bounded):
    shift_ref, q_ref = refs[0], refs[1]
    kv_refs = refs[2:2 + 2 * n_kv]
    o_ref = refs[2 + 2 * n_kv]
    m_scr, l_scr, acc_scr = refs[3 + 2 * n_kv:]
    if not bounded:
        m_scr[...] = jnp.full(m_scr.shape, -jnp.inf, F32)
    l_scr[...] = jnp.zeros(l_scr.shape, F32)
    acc_scr[...] = jnp.zeros(acc_scr.shape, F32)
    q = q_ref[0, 0]
    shift = shift_ref[0] if bounded else None
    _kv_loop(kv_refs, tk, lambda k, v: _softmax_step(_qk(q, k), v, shift, m_scr, l_scr, acc_scr),
             ATTN_UNROLL if bounded else 1)
    o_ref[0] = (acc_scr[...] / _row_sum(l_scr, bounded)).astype(BF16)


def _mla_attn(q, kvs, shift, bounded):
    B, H, N, DQ = q.shape
    tq = _tile(N, 1024)
    in_specs = [pl.BlockSpec(memory_space=pltpu.SMEM),
                pl.BlockSpec((1, 1, tq, DQ), lambda b, h, i: (b, h, i, 0))]
    args = [shift, q]
    for k, v in kvs:
        nk = k.shape[2]
        in_specs += [pl.BlockSpec((1, 1, nk, DQ), lambda b, h, i: (b, h, 0, 0)),
                     pl.BlockSpec((1, 1, nk, MLA_V), lambda b, h, i: (b, h, 0, 0))]
        args += [k, v]
    stat = pltpu.VMEM((tq, LANE), F32)
    return pl.pallas_call(
        functools.partial(_mla_attn_kernel, n_kv=len(kvs), tk=ATTN_TK, bounded=bounded),
        grid=(B, H, N // tq),
        in_specs=in_specs,
        out_specs=pl.BlockSpec((1, tq, MLA_V), lambda b, h, i: (b, i, h)),
        out_shape=jax.ShapeDtypeStruct((B, N, H * MLA_V), BF16),
        scratch_shapes=[stat, stat, stat],
        compiler_params=_cparams("parallel", "parallel", "arbitrary"),
        name="mla_attention",
    )(*args)


def _diff_attn_kernel(*refs, n_kv, tk, lam_init, bounded):
    shift_ref, q_ref = refs[0], refs[1]
    kv_refs = refs[2:2 + 2 * n_kv]
    lamp_ref, gout_ref, o_ref = refs[2 + 2 * n_kv:5 + 2 * n_kv]
    m0, l0, acc0, m1, l1, acc1 = refs[5 + 2 * n_kv:]
    if not bounded:
        for m in (m0, m1):
            m[...] = jnp.full(m.shape, -jnp.inf, F32)
    for z in (l0, l1, acc0, acc1):
        z[...] = jnp.zeros(z.shape, F32)
    q = q_ref[0, 0]
    first = lax.broadcasted_iota(jnp.int32, q.shape, 1) < DIFF_D
    zero = jnp.zeros(q.shape, BF16)
    qa = jnp.where(first, q, zero)
    qb = jnp.where(first, zero, q)
    shift = shift_ref[0] if bounded else None

    def step(k, v):
        _softmax_step(_qk(qa, k), v, shift, m0, l0, acc0)
        _softmax_step(_qk(qb, k), v, shift, m1, l1, acc1)

    _kv_loop(kv_refs, tk, step, ATTN_UNROLL if bounded else 1)

    lp = lamp_ref[...]
    lam = (jnp.exp(jnp.sum(lp[0:1] * lp[1:2], keepdims=True))
           - jnp.exp(jnp.sum(lp[2:3] * lp[3:4], keepdims=True)) + lam_init)
    o = acc0[...] / _row_sum(l0, bounded) - lam * (acc1[...] / _row_sum(l1, bounded))
    on = (o * lax.rsqrt(jnp.mean(o * o, axis=-1, keepdims=True) + EPS)) * gout_ref[...]
    o_ref[0] = (on * (1.0 - lam_init)).astype(BF16)


def _diff_attn(q, kvs, lam_p, g_out, lam_init, shift, bounded):
    B, H, N, DQ = q.shape
    tq = _tile(N, 1024)
    in_specs = [pl.BlockSpec(memory_space=pltpu.SMEM),
                pl.BlockSpec((1, 1, tq, DQ), lambda b, h, i: (b, h, i, 0))]
    args = [shift, q]
    for k, v in kvs:
        nk = k.shape[2]
        kv_spec = pl.BlockSpec((1, 1, nk, LANE), lambda b, h, i: (b, h, 0, 0))
        in_specs += [kv_spec, kv_spec]
        args += [k, v]
    in_specs += [pl.BlockSpec(lam_p.shape, lambda b, h, i: (0, 0)),
                 pl.BlockSpec((1, DIFF_V), lambda b, h, i: (0, 0))]
    args += [lam_p, g_out]
    stat = pltpu.VMEM((tq, LANE), F32)
    return pl.pallas_call(
        functools.partial(_diff_attn_kernel, n_kv=len(kvs), tk=ATTN_TK, lam_init=lam_init, bounded=bounded),
        grid=(B, H, N // tq),
        in_specs=in_specs,
        out_specs=pl.BlockSpec((1, tq, DIFF_V), lambda b, h, i: (b, i, h)),
        out_shape=jax.ShapeDtypeStruct((B, N, H * DIFF_V), BF16),
        scratch_shapes=[stat] * 6,
        compiler_params=_cparams("parallel", "parallel", "arbitrary"),
        name="diff_attention",
    )(*args)


def _merge_kernel(a_ref, b_ref, c_ref, gt_ref, p_ref, o_ref, acc_scr):
    j = pl.program_id(2)

    def contrib(br_ref):
        return jax.nn.sigmoid(gt_ref[0]) * jnp.dot(br_ref[0], p_ref[0], preferred_element_type=F32)

    @pl.when(j == 0)
    def _():
        acc_scr[...] = contrib(a_ref)

    @pl.when(j == 1)
    def _():
        acc_scr[...] += contrib(b_ref)

    @pl.when(j == 2)
    def _():
        o_ref[0] = (acc_scr[...] + contrib(c_ref)).astype(BF16)


def _merge(a, b, c, proj, bproj):
    B, N, W = a.shape
    D = bproj.shape[2]
    tm = _tile(N, 512)
    br = pl.BlockSpec((1, tm, W), lambda b_, i, j: (b_, i, 0))
    return pl.pallas_call(
        _merge_kernel,
        grid=(B, N // tm, N_BRANCH),
        in_specs=[br, br, br,
                  pl.BlockSpec((1, tm, D), lambda b_, i, j: (b_, i, j)),
                  pl.BlockSpec((1, W, D), lambda b_, i, j: (j, 0, 0))],
        out_specs=pl.BlockSpec((1, tm, D), lambda b_, i, j: (b_, i, 0)),
        out_shape=jax.ShapeDtypeStruct((B, N, D), BF16),
        scratch_shapes=[pltpu.VMEM((tm, D), F32)],
        compiler_params=_cparams("parallel", "parallel", "arbitrary"),
        name="branch_merge",
    )(a, b, c, proj, bproj)


def _outproj_kernel(x_ref, y_ref, w_ref, m2_ref, o_ref):
    o_ref[0] = x_ref[0] + m2_ref[0] * jnp.dot(y_ref[0], w_ref[...], preferred_element_type=F32)


def _outproj(x, y, w, m2):
    B, N, D = x.shape
    tm = _tile(N, 512)
    tok = pl.BlockSpec((1, tm, D), lambda b, i: (b, i, 0))
    return pl.pallas_call(
        _outproj_kernel,
        grid=(B, N // tm),
        in_specs=[tok, tok, pl.BlockSpec((D, D), lambda b, i: (0, 0)),
                  pl.BlockSpec((1, 1, D), lambda b, i: (b, 0, 0))],
        out_specs=tok,
        out_shape=jax.ShapeDtypeStruct((B, N, D), F32),
        compiler_params=_cparams("parallel", "parallel"),
        name="mixer_out_proj",
    )(x, y, w, m2)


def _rope_tables(n_tokens, identity):
    half = MLA_ROPE // 2
    if identity:
        return jnp.ones((n_tokens, LANE), F32), jnp.zeros((n_tokens, LANE), F32)
    rows = n_tokens // GRID_W
    axis_dim = MLA_ROPE // 2
    inv_freq = ROPE_THETA ** (-jnp.arange(0, axis_dim, 2, dtype=F32) / axis_dim)
    nf = axis_dim // 2
    row_ang = jnp.arange(rows, dtype=F32)[:, None] * inv_freq
    col_ang = jnp.arange(GRID_W, dtype=F32)[:, None] * inv_freq
    ang = jnp.concatenate([jnp.broadcast_to(row_ang[:, None, :], (rows, GRID_W, nf)),
                           jnp.broadcast_to(col_ang[None, :, :], (rows, GRID_W, nf))], axis=-1)
    ang = ang.reshape(rows * GRID_W, axis_dim)
    cos, sin = jnp.cos(ang), jnp.sin(ang)
    assert half == axis_dim
    return jnp.concatenate([cos, cos, cos, cos], axis=1), jnp.concatenate([-sin, sin, -sin, sin], axis=1)


def _lane_matrices():
    i = jnp.arange(LANE)
    half = MLA_ROPE // 2
    partner = jnp.where(i % MLA_ROPE < half, i + half, i - half)
    perm = (i[:, None] == partner[None, :]).astype(BF16)
    seg = (i[:, None] // DIFF_D == i[None, :] // DIFF_D).astype(BF16)
    return perm, seg


def _in_proj_layout(D, W):
    order = [("gt", N_BRANCH * D), ("rx", W), ("rg", W), ("dq", DIFF_HEADS * 2 * DIFF_D),
             ("dk", DIFF_HEADS * 2 * DIFF_D), ("dv", DIFF_HEADS * DIFF_V)]
    cols, off = {}, 0
    for name, width in order:
        if name != "gt":
            off = -(-off // width) * width
        cols[name] = off
        off += width
    return cols, off


def kernel(x, c, ctx, c_ctx, w_mod, b_mod, norm_g, ffn_w_gate, ffn_w_up, ffn_w_down, w_in, lru_conv_w,
           lru_conv_b, lru_wa, lru_ba, lru_wx, lru_bx, lru_lambda, mla_qc_norm, mla_kvc_norm, mla_w_uq,
           mla_w_ukv, mla_q_norm, mla_k_norm, diff_q_norm, diff_k_norm, diff_lambda, diff_out_norm,
           branch_proj, w_out):
    B, N, D = x.shape
    NC = ctx.shape[1]
    L = w_mod.shape[0]
    W = lru_conv_w.shape[-1]
    QR, KR = mla_qc_norm.shape[-1], mla_kvc_norm.shape[-1]
    dqw = DIFF_HEADS * 2 * DIFF_D

    sizes = dict(rx=W, rg=W, qc=QR, kvc=KR, kr=MLA_ROPE, dq=dqw, dk=dqw, dv=DIFF_HEADS * DIFF_V,
                 gt=N_BRANCH * D)
    src, off = {}, 0
    for name in ("rx", "rg", "qc", "kvc", "kr", "dq", "dk", "dv", "gt"):
        src[name] = off
        off += sizes[name]
    cols, off = _in_proj_layout(D, W)
    for name, width in (("qc", QR), ("kvc", KR), ("kr", LANE)):
        off = -(-off // width) * width
        cols[name] = off
        off += width
    NP = -(-off // 2048) * 2048 if off > 2048 else off

    def pad_w_in(w):
        out = jnp.zeros((D, NP), BF16)
        for name in sizes:
            out = lax.dynamic_update_slice(out, w[:, src[name]:src[name] + sizes[name]].astype(BF16),
                                           (0, cols[name]))
        return out

    def pad_heads(w, width):
        K = w.shape[0]
        w3 = w.reshape(K, MLA_HEADS, width)
        return jnp.pad(w3, ((0, 0), (0, 0), (0, MLA_QK_PAD - width))).reshape(K, MLA_HEADS * MLA_QK_PAD)

    R = -(-(B + 1) // SUBLANE) * SUBLANE
    cpad = jnp.zeros((R, D), F32).at[:B].set(c).at[B].set(c_ctx)
    mod = _compute_mod(cpad, w_mod, b_mod).reshape(L, R, N_MOD, D)

    perm, seg = _lane_matrices()
    tabs_l = _rope_tables(N, identity=False) + (perm,)
    tabs_c = _rope_tables(NC, identity=True) + (perm,)

    xl, xc = x, ctx.reshape(1, B * NC, D)
    for l in range(L):
        last = l == L - 1
        lam_init = 0.8 - 0.6 * math.exp(-0.3 * l)
        mod_l = mod[l, :B]
        mod_c = mod[l, B:B + 1]

        def rows(m, s):
            return m[:, 3 * s:3 * s + 1], m[:, 3 * s + 1:3 * s + 2], m[:, 3 * s + 2:3 * s + 3]

        ng = norm_g[l]
        f1 = (ffn_w_gate[l, 0].astype(BF16), ffn_w_up[l, 0].astype(BF16), ffn_w_down[l, 0].astype(BF16))
        f2 = (ffn_w_gate[l, 1].astype(BF16), ffn_w_up[l, 1].astype(BF16), ffn_w_down[l, 1].astype(BF16))
        w_in_p = pad_w_in(w_in[l])
        wuq = pad_heads(mla_w_uq[l], MLA_QK).astype(BF16)
        wukv = mla_w_ukv[l].astype(BF16)
        gq = jnp.pad(mla_q_norm[l], (0, MLA_QK_PAD - MLA_QK)).reshape(1, MLA_QK_PAD)
        gk = jnp.pad(mla_k_norm[l], (0, MLA_QK_PAD - MLA_QK)).reshape(1, MLA_QK_PAD)
        gqc = mla_qc_norm[l].reshape(1, QR)
        gkvc = mla_kvc_norm[l].reshape(1, KR)
        gdq = jnp.tile(diff_q_norm[l], 2).reshape(1, LANE)
        gdk = jnp.tile(diff_k_norm[l], 2).reshape(1, LANE)
        gdo = diff_out_norm[l].reshape(1, DIFF_V)
        bproj = branch_proj[l].astype(BF16)
        wo = w_out[l].astype(BF16)
        lru = dict(cw=lru_conv_w[l], cb=lru_conv_b[l].reshape(1, W))

        bound_m = math.sqrt(MLA_QK) * jnp.max(jnp.abs(mla_q_norm[l])) * jnp.max(jnp.abs(mla_k_norm[l]))
        bound_d = math.sqrt(DIFF_D) * jnp.max(jnp.abs(diff_q_norm[l])) * jnp.max(jnp.abs(diff_k_norm[l]))

        def mla_attend(q, kvs):
            shift = (bound_m * LOG2E).reshape(1)
            return lax.cond(bound_m <= MAX_SCORE_BOUND,
                            lambda: _mla_attn(q, kvs, shift, True), lambda: _mla_attn(q, kvs, shift, False))

        def diff_attend(q, kvs):
            shift = (bound_d * LOG2E).reshape(1)
            args = (diff_lambda[l], gdo, lam_init, shift)
            return lax.cond(bound_d <= MAX_SCORE_BOUND,
                            lambda: _diff_attn(q, kvs, *args, True), lambda: _diff_attn(q, kvs, *args, False))

        m0, m1, m2 = rows(mod_l, 0)
        xl = _ffn(xl, ng[0:1], m0, m1, m2, *f1)
        m0, m1, m2 = rows(mod_c, 0)
        xc = _ffn(xc, ng[0:1], m0, m1, m2, *f1)

        m0, m1, gate_l = rows(mod_l, 1)
        proj_l = _inproj(xl, ng[1:2], m0, m1, w_in_p)
        m0, m1, gate_c = rows(mod_c, 1)
        proj_c = _inproj(xc, ng[1:2], m0, m1, w_in_p).reshape(B, NC, NP)

        h_zero = jnp.zeros((B, 1, W), F32)
        rx, rg = cols["rx"] // W, cols["rg"] // W
        prm = [dict(wa=lru_wa[l, d].astype(BF16), ba=lru_ba[l, d].reshape(1, W), wx=lru_wx[l, d].astype(BF16),
                    bx=lru_bx[l, d].reshape(1, W), lam=lru_lambda[l, d].reshape(1, W)) for d in range(2)]
        hf_c, xc_c, fin_c = _lru_fwd(proj_c, rx, h_zero, lru["cw"], lru["cb"], W=W, **prm[0])
        hf_l, xc_l, _ = _lru_fwd(proj_l, rx, fin_c, lru["cw"], lru["cb"], W=W, **prm[0])
        a_ctx, fin_c = _lru_rev(xc_c, hf_c, proj_c, rg, h_zero, W=W, **prm[1])
        a_lat, _ = _lru_rev(xc_l, hf_l, proj_l, rg, fin_c, W=W, **prm[1])

        q_ml, k_ml, v_ml = _mla_prep(proj_l, cols, gqc, gkvc, wuq, wukv, gq, gk, *tabs_l)
        q_mc, k_mc, v_mc = _mla_prep(proj_c, cols, gqc, gkvc, wuq, wukv, gq, gk, *tabs_c)
        b_lat = mla_attend(q_ml, [(k_ml, v_ml), (k_mc, v_mc)])

        q_dl, k_dl, v_dl = _diff_prep(proj_l, cols, gdq, gdk, *tabs_l, seg)
        q_dc, k_dc, v_dc = _diff_prep(proj_c, cols, gdq, gdk, *tabs_c, seg)
        c_lat = diff_attend(q_dl, [(k_dl, v_dl), (k_dc, v_dc)])

        y_lat = _merge(a_lat, b_lat, c_lat, proj_l, bproj)
        xl = _outproj(xl, y_lat, wo, gate_l)

        m0, m1, m2 = rows(mod_l, 2)
        xl = _ffn(xl, ng[2:3], m0, m1, m2, *f2)

        if not last:
            b_ctx = mla_attend(q_mc, [(k_mc, v_mc)])
            c_ctx_o = diff_attend(q_dc, [(k_dc, v_dc)])
            y_ctx = _merge(a_ctx, b_ctx, c_ctx_o, proj_c, bproj)
            xc = _outproj(xc, y_ctx.reshape(1, B * NC, D), wo, gate_c)
            m0, m1, m2 = rows(mod_c, 2)
            xc = _ffn(xc, ng[2:3], m0, m1, m2, *f2)
    return xl
```

```python
import functools
import math

import jax
import jax.numpy as jnp
from jax import lax
from jax.experimental import pallas as pl
from jax.experimental.pallas import tpu as pltpu

F32 = jnp.float32
BF16 = jnp.bfloat16

EPS = 1e-6
ROPE_THETA = 10000.0
GRID_W = 64
N_BRANCH = 3
N_MOD = 9
LRU_BLOCKS = 8
CONV_W = 4
LRU_C = 8.0
MLA_HEADS = 8
MLA_NOPE = 128
MLA_ROPE = 64
MLA_V = 128
MLA_QK = MLA_NOPE + MLA_ROPE
DIFF_HEADS = 8
DIFF_D = 64
DIFF_V = 2 * DIFF_D

LANE = 128
SUBLANE = 8
MLA_QK_PAD = 256
VMEM_LIMIT = 56 * 1024 * 1024
FFN_FK = 512
NORM_ROWS = 16
LOG2E = math.log2(math.e)
ATTN_TK = 1024
ATTN_UNROLL = 8
MAX_SCORE_BOUND = 30.0


def _cparams(*sem):
    return pltpu.CompilerParams(dimension_semantics=sem, vmem_limit_bytes=VMEM_LIMIT)


def _tile(n, pref):
    if n <= pref:
        return n
    t = pref - pref % SUBLANE
    while t > SUBLANE and n % t:
        t -= SUBLANE
    assert n % t == 0, (n, pref)
    return t


def _sigmoid(x):
    return 0.5 * jnp.tanh(0.5 * x) + 0.5


def _norm_mod_into(h_scr, x_ref, g, m0, m1):
    tm = h_scr.shape[0]
    rows = NORM_ROWS if tm % NORM_ROWS == 0 else tm
    gm = g * (1.0 + m0)

    def body(c, carry):
        r0 = pl.multiple_of(c * rows, rows)
        x = x_ref[0, pl.ds(r0, rows), :]
        ms = jnp.mean(x * x, axis=-1, keepdims=True)
        h_scr[pl.ds(r0, rows), :] = ((x * lax.rsqrt(ms + EPS)) * gm + m1).astype(BF16)
        return carry

    lax.fori_loop(0, tm // rows, body, 0, unroll=math.gcd(tm // rows, 16))


def _mod_kernel(c_ref, w_ref, b_ref, o_ref):
    c = c_ref[...]
    sc = c * jax.nn.sigmoid(c)
    o_ref[0] = jnp.dot(sc, w_ref[0], preferred_element_type=F32) + b_ref[0]


def _compute_mod(cpad, w_mod, b_mod):
    L, D, ND = w_mod.shape
    R = cpad.shape[0]
    tn = _tile(ND, 1024)
    return pl.pallas_call(
        _mod_kernel,
        grid=(L, ND // tn),
        in_specs=[pl.BlockSpec((R, D), lambda l, j: (0, 0)),
                  pl.BlockSpec((1, D, tn), lambda l, j: (l, 0, j)),
                  pl.BlockSpec((1, 1, tn), lambda l, j: (l, 0, j))],
        out_specs=pl.BlockSpec((1, R, tn), lambda l, j: (l, 0, j)),
        out_shape=jax.ShapeDtypeStruct((L, R, ND), F32),
        compiler_params=_cparams("parallel", "parallel"),
        name="adaln_mod",
    )(cpad, w_mod, b_mod.reshape(L, 1, ND))


def _ffn_kernel(x_ref, g_ref, m0_ref, m1_ref, m2_ref, wg_ref, wu_ref, wd_ref, o_ref, h_scr, *, tail):
    f = pl.program_id(2)
    last = pl.num_programs(2) - 1
    fk = wg_ref.shape[1]

    @pl.when(f == 0)
    def _():
        _norm_mod_into(h_scr, x_ref, g_ref[...], m0_ref[0], m1_ref[0])
        o_ref[0] = jnp.zeros(o_ref.shape[1:], F32)

    def chunk(width):
        h = h_scr[...]
        g = jnp.dot(h, wg_ref[:, 0:width], preferred_element_type=F32)
        u = jnp.dot(h, wu_ref[:, 0:width], preferred_element_type=F32)
        a = (g * jax.nn.sigmoid(g)) * u
        o_ref[0] += jnp.dot(a.astype(BF16), wd_ref[0:width, :], preferred_element_type=F32)

    if tail == fk:
        chunk(fk)
    else:
        pl.when(f < last)(lambda: chunk(fk))
        pl.when(f == last)(lambda: chunk(tail))

    @pl.when(f == last)
    def _():
        o_ref[0] = x_ref[0] + (0.5 * m2_ref[0]) * o_ref[0]


def _ffn(x, g, m0, m1, m2, wg, wu, wd):
    B, N, D = x.shape
    F = wg.shape[1]
    tm = _tile(N, 512)
    fk = min(F, FFN_FK)
    nf = -(-F // fk)
    tail = F - (nf - 1) * fk
    tok = pl.BlockSpec((1, tm, D), lambda b, i, f: (b, i, 0))
    row = pl.BlockSpec((1, 1, D), lambda b, i, f: (b, 0, 0))
    return pl.pallas_call(
        functools.partial(_ffn_kernel, tail=tail),
        grid=(B, N // tm, nf),
        in_specs=[tok, pl.BlockSpec((1, D), lambda b, i, f: (0, 0)), row, row, row,
                  pl.BlockSpec((D, fk), lambda b, i, f: (0, f)),
                  pl.BlockSpec((D, fk), lambda b, i, f: (0, f)),
                  pl.BlockSpec((fk, D), lambda b, i, f: (f, 0))],
        out_specs=tok,
        out_shape=jax.ShapeDtypeStruct((B, N, D), F32),
        scratch_shapes=[pltpu.VMEM((tm, D), BF16)],
        compiler_params=_cparams("parallel", "parallel", "arbitrary"),
        name="swiglu_half_step",
    )(x, g, m0, m1, m2, wg, wu, wd)


def _inproj_kernel(x_ref, g_ref, m0_ref, m1_ref, w_ref, o_ref, h_scr):
    @pl.when(pl.program_id(2) == 0)
    def _():
        _norm_mod_into(h_scr, x_ref, g_ref[...], m0_ref[0], m1_ref[0])

    o_ref[0] = jnp.dot(h_scr[...], w_ref[...], preferred_element_type=F32)


def _inproj(x, g, m0, m1, w):
    B, N, D = x.shape
    NP = w.shape[1]
    tm = _tile(N, 1024)
    tn = _tile(NP, 1024)
    row = pl.BlockSpec((1, 1, D), lambda b, i, j: (b, 0, 0))
    return pl.pallas_call(
        _inproj_kernel,
        grid=(B, N // tm, NP // tn),
        in_specs=[pl.BlockSpec((1, tm, D), lambda b, i, j: (b, i, 0)),
                  pl.BlockSpec((1, D), lambda b, i, j: (0, 0)), row, row,
                  pl.BlockSpec((D, tn), lambda b, i, j: (0, j))],
        out_specs=pl.BlockSpec((1, tm, tn), lambda b, i, j: (b, i, j)),
        out_shape=jax.ShapeDtypeStruct((B, N, NP), F32),
        scratch_shapes=[pltpu.VMEM((tm, D), BF16)],
        compiler_params=_cparams("parallel", "parallel", "arbitrary"),
        name="mixer_in_proj",
    )(x, g, m0, m1, w)


def _lru_gates_and_scan(xc_of, wa_ref, ba_ref, wx_ref, bx_ref, lam_ref, h0_ref, store_h, hfin_ref,
                        a_scr, u_scr, carry_scr, *, reverse, tm, W):
    H = SUBLANE

    @pl.when(pl.program_id(1) == 0)
    def _():
        carry_scr[...] = h0_ref[0]

    sp = jax.nn.softplus(-lam_ref[...])
    lb = W // LRU_BLOCKS
    for n in range(LRU_BLOCKS):
        sl = slice(n * lb, (n + 1) * lb)
        xc = xc_of(sl)
        xb = xc.astype(BF16)
        r = _sigmoid(jnp.dot(xb, wa_ref[n], preferred_element_type=F32) + ba_ref[:, sl])
        ig = _sigmoid(jnp.dot(xb, wx_ref[n], preferred_element_type=F32) + bx_ref[:, sl])
        a = jnp.exp((-LRU_C) * r * sp[:, sl])
        a_scr[:, sl] = a
        u_scr[:, sl] = jnp.sqrt(1.0 - a * a) * (ig * xc)

    rowi = lax.broadcasted_iota(jnp.int32, (H, W), 0)
    ng = tm // H

    def group(gi, h):
        g = (ng - 1 - gi) if reverse else gi
        r0 = pl.multiple_of(g * H, H)
        a = a_scr[pl.ds(r0, H), :]
        u = u_scr[pl.ds(r0, H), :]
        for s in (1, 2, 4):
            sh = (H - s) if reverse else s
            valid = (rowi < H - s) if reverse else (rowi >= s)
            a_s = pltpu.roll(a, sh, 0)
            u_s = pltpu.roll(u, sh, 0)
            u = jnp.where(valid, a * u_s + u, u)
            a = jnp.where(valid, a * a_s, a)
        hh = a * h + u
        store_h(r0, hh)
        return hh[0:1, :] if reverse else hh[H - 1:H, :]

    h_last = lax.fori_loop(0, ng, group, carry_scr[...])
    carry_scr[...] = h_last
    hfin_ref[0] = h_last


def _lru_fwd_kernel(x_ref, xp_ref, xn_ref, cw_ref, cb_ref, wa_ref, ba_ref, wx_ref, bx_ref, lam_ref, h0_ref,
                    h_ref, xc_ref, hfin_ref, xs_scr, a_scr, u_scr, carry_scr, *, nt, tm, W):
    ti = pl.program_id(1)
    H = SUBLANE
    zero = jnp.zeros((H, W), F32)
    xs_scr[0:H, :] = jnp.where(ti > 0, xp_ref[0], zero)
    xs_scr[H:H + tm, :] = x_ref[0]
    xs_scr[H + tm:2 * H + tm, :] = jnp.where(ti < nt - 1, xn_ref[0], zero)
    left = CONV_W // 2

    def conv(sl):
        xc = cb_ref[:, sl] + xs_scr[H - left:H - left + tm, sl] * cw_ref[0:1, sl]
        for j in range(1, CONV_W):
            xc = xc + xs_scr[H - left + j:H - left + j + tm, sl] * cw_ref[j:j + 1, sl]
        xc_ref[0, :, sl] = xc
        return xc

    def store_h(r0, hh):
        h_ref[0, pl.ds(r0, H), :] = hh

    _lru_gates_and_scan(conv, wa_ref, ba_ref, wx_ref, bx_ref, lam_ref, h0_ref, store_h, hfin_ref,
                        a_scr, u_scr, carry_scr, reverse=False, tm=tm, W=W)


def _lru_rev_kernel(xc_ref, hf_ref, rg_ref, wa_ref, ba_ref, wx_ref, bx_ref, lam_ref, h0_ref,
                    y_ref, hfin_ref, hb_scr, a_scr, u_scr, carry_scr, *, tm, W):
    def store_h(r0, hh):
        hb_scr[pl.ds(r0, SUBLANE), :] = hh

    _lru_gates_and_scan(lambda sl: xc_ref[0, :, sl], wa_ref, ba_ref, wx_ref, bx_ref, lam_ref, h0_ref,
                        store_h, hfin_ref, a_scr, u_scr, carry_scr, reverse=True, tm=tm, W=W)
    lb = W // LRU_BLOCKS
    for n in range(LRU_BLOCKS):
        sl = slice(n * lb, (n + 1) * lb)
        y_ref[0, :, sl] = ((hf_ref[0, :, sl] + hb_scr[:, sl]) * jax.nn.gelu(rg_ref[0, :, sl])).astype(BF16)


def _lru_specs(W):
    vec = pl.BlockSpec((1, W), lambda b, i: (0, 0))
    blk = pl.BlockSpec((LRU_BLOCKS, W // LRU_BLOCKS, W // LRU_BLOCKS), lambda b, i: (0, 0, 0))
    state = pl.BlockSpec((1, 1, W), lambda b, i: (b, 0, 0))
    return vec, blk, state


def _lru_fwd(proj, col_blk, h0, cw, cb, wa, ba, wx, bx, lam, *, W):
    B, N, _ = proj.shape
    tm = _tile(N, 256)
    nt = N // tm
    r8 = tm // SUBLANE
    nb8 = N // SUBLANE
    vec, blk, state = _lru_specs(W)
    tok = pl.BlockSpec((1, tm, W), lambda b, i: (b, i, 0))
    full = jax.ShapeDtypeStruct((B, N, W), F32)
    return pl.pallas_call(
        functools.partial(_lru_fwd_kernel, nt=nt, tm=tm, W=W),
        grid=(B, nt),
        in_specs=[pl.BlockSpec((1, tm, W), lambda b, i: (b, i, col_blk)),
                  pl.BlockSpec((1, SUBLANE, W), lambda b, i: (b, jnp.maximum(i * r8 - 1, 0), col_blk)),
                  pl.BlockSpec((1, SUBLANE, W), lambda b, i: (b, jnp.minimum((i + 1) * r8, nb8 - 1), col_blk)),
                  pl.BlockSpec((CONV_W, W), lambda b, i: (0, 0)), vec, blk, vec, blk, vec, vec, state],
        out_specs=[tok, tok, state],
        out_shape=[full, full, jax.ShapeDtypeStruct((B, 1, W), F32)],
        scratch_shapes=[pltpu.VMEM((tm + 2 * SUBLANE, W), F32), pltpu.VMEM((tm, W), F32),
                        pltpu.VMEM((tm, W), F32), pltpu.VMEM((1, W), F32)],
        compiler_params=_cparams("parallel", "arbitrary"),
        name="rglru_fwd",
    )(proj, proj, proj, cw, cb, wa, ba, wx, bx, lam, h0)


def _lru_rev(xc, hf, proj, gate_blk, h0, wa, ba, wx, bx, lam, *, W):
    B, N, _ = xc.shape
    tm = _tile(N, 256)
    nt = N // tm
    vec, blk, state = _lru_specs(W)
    tok = pl.BlockSpec((1, tm, W), lambda b, i: (b, nt - 1 - i, 0))
    return pl.pallas_call(
        functools.partial(_lru_rev_kernel, tm=tm, W=W),
        grid=(B, nt),
        in_specs=[tok, tok, pl.BlockSpec((1, tm, W), lambda b, i: (b, nt - 1 - i, gate_blk)),
                  blk, vec, blk, vec, vec, state],
        out_specs=[tok, state],
        out_shape=[jax.ShapeDtypeStruct((B, N, W), BF16), jax.ShapeDtypeStruct((B, 1, W), F32)],
        scratch_shapes=[pltpu.VMEM((tm, W), F32), pltpu.VMEM((tm, W), F32),
                        pltpu.VMEM((tm, W), F32), pltpu.VMEM((1, W), F32)],
        compiler_params=_cparams("parallel", "arbitrary"),
        name="rglru_rev_gate",
    )(xc, hf, proj, wa, ba, wx, bx, lam, h0)


def _rope(x, cosf, sins, perm):
    return x * cosf + jnp.dot(x.astype(BF16), perm, preferred_element_type=F32) * sins


def _mla_prep_kernel(qc_ref, kvc_ref, kr_ref, gqc_ref, gkvc_ref, wuq_ref, wukv_ref, gq_ref, gk_ref,
                     cos_ref, sin_ref, perm_ref, q_ref, k_ref, v_ref):
    cosf, sins, perm = cos_ref[...], sin_ref[...], perm_ref[...]
    gqs = gq_ref[...] * (MLA_QK ** -0.5 * LOG2E)

    qc = qc_ref[0]
    qn = (qc * lax.rsqrt(jnp.mean(qc * qc, axis=-1, keepdims=True) + EPS)) * gqc_ref[...]
    qf = jnp.dot(qn.astype(BF16), wuq_ref[...], preferred_element_type=F32)
    kvc = kvc_ref[0]
    kvn = (kvc * lax.rsqrt(jnp.mean(kvc * kvc, axis=-1, keepdims=True) + EPS)) * gkvc_ref[...]
    kvf = jnp.dot(kvn.astype(BF16), wukv_ref[...], preferred_element_type=F32)

    kr = kr_ref[0]
    ss_kr = jnp.sum(kr * kr, axis=-1, keepdims=True)
    kr_rot = _rope(kr * gk_ref[:, LANE:], cosf, sins, perm)

    for h in range(MLA_HEADS):
        o = h * MLA_QK_PAD
        qa = qf[:, o:o + LANE]
        qb = qf[:, o + LANE:o + 2 * LANE]
        ss = jnp.sum(qa * qa + qb * qb, axis=-1, keepdims=True)
        rinv = lax.rsqrt(ss * (1.0 / MLA_QK) + EPS)
        q_ref[0, h, :, 0:LANE] = ((qa * rinv) * gqs[:, 0:LANE]).astype(BF16)
        q_ref[0, h, :, LANE:] = _rope((qb * rinv) * gqs[:, LANE:], cosf, sins, perm).astype(BF16)

        kn = kvf[:, o:o + LANE]
        ssk = jnp.sum(kn * kn, axis=-1, keepdims=True) + ss_kr
        rk = lax.rsqrt(ssk * (1.0 / MLA_QK) + EPS)
        k_ref[0, h, :, 0:LANE] = ((kn * rk) * gk_ref[:, 0:LANE]).astype(BF16)
        k_ref[0, h, :, LANE:] = (kr_rot * rk).astype(BF16)
        v_ref[0, h] = kvf[:, o + LANE:o + 2 * LANE].astype(BF16)


def _mla_prep(proj, cols, gqc, gkvc, wuq, wukv, gq, gk, cosf, sins, perm):
    B, N, _ = proj.shape
    tm = _tile(N, 256)
    QR, KR = gqc.shape[1], gkvc.shape[1]
    HP = MLA_HEADS * MLA_QK_PAD
    tab = pl.BlockSpec((tm, LANE), lambda b, i: (i, 0))

    def full(a):
        return pl.BlockSpec(a.shape, lambda b, i: (0,) * a.ndim)

    hd = lambda w: pl.BlockSpec((1, MLA_HEADS, tm, w), lambda b, i: (b, 0, i, 0))
    return pl.pallas_call(
        _mla_prep_kernel,
        grid=(B, N // tm),
        in_specs=[pl.BlockSpec((1, tm, QR), lambda b, i: (b, i, cols["qc"] // QR)),
                  pl.BlockSpec((1, tm, KR), lambda b, i: (b, i, cols["kvc"] // KR)),
                  pl.BlockSpec((1, tm, LANE), lambda b, i: (b, i, cols["kr"] // LANE)),
                  full(gqc), full(gkvc), full(wuq), full(wukv), full(gq), full(gk), tab, tab, full(perm)],
        out_specs=[hd(MLA_QK_PAD), hd(MLA_QK_PAD), hd(MLA_V)],
        out_shape=[jax.ShapeDtypeStruct((B, MLA_HEADS, N, MLA_QK_PAD), BF16),
                   jax.ShapeDtypeStruct((B, MLA_HEADS, N, MLA_QK_PAD), BF16),
                   jax.ShapeDtypeStruct((B, MLA_HEADS, N, MLA_V), BF16)],
        compiler_params=_cparams("parallel", "parallel"),
        name="mla_qkv_prep",
    )(proj, proj, proj, gqc, gkvc, wuq, wukv, gq, gk, cosf, sins, perm)


def _diff_prep_kernel(dq_ref, dk_ref, dv_ref, gq_ref, gk_ref, cos_ref, sin_ref, perm_ref, seg_ref,
                      q_ref, k_ref, v_ref):
    cosf, sins, perm, seg = cos_ref[...], sin_ref[...], perm_ref[...], seg_ref[...]
    gqs = gq_ref[...] * (DIFF_D ** -0.5 * LOG2E)
    gk = gk_ref[...]

    def norm_rope(x, g):
        ss = jnp.dot((x * x).astype(BF16), seg, preferred_element_type=F32)
        return _rope((x * lax.rsqrt(ss * (1.0 / DIFF_D) + EPS)) * g, cosf, sins, perm)

    for h in range(DIFF_HEADS):
        sl = slice(h * LANE, (h + 1) * LANE)
        q_ref[0, h] = norm_rope(dq_ref[0, :, sl], gqs).astype(BF16)
        k_ref[0, h] = norm_rope(dk_ref[0, :, sl], gk).astype(BF16)
        v_ref[0, h] = dv_ref[0, :, sl].astype(BF16)


def _diff_prep(proj, cols, gq2, gk2, cosf, sins, perm, seg):
    B, N, _ = proj.shape
    tm = _tile(N, 256)
    HW = DIFF_HEADS * LANE
    tab = pl.BlockSpec((tm, LANE), lambda b, i: (i, 0))
    mat = pl.BlockSpec((LANE, LANE), lambda b, i: (0, 0))
    g = pl.BlockSpec((1, LANE), lambda b, i: (0, 0))
    hd = pl.BlockSpec((1, DIFF_HEADS, tm, LANE), lambda b, i: (b, 0, i, 0))
    shp = jax.ShapeDtypeStruct((B, DIFF_HEADS, N, LANE), BF16)
    return pl.pallas_call(
        _diff_prep_kernel,
        grid=(B, N // tm),
        in_specs=[pl.BlockSpec((1, tm, HW), lambda b, i: (b, i, cols["dq"] // HW)),
                  pl.BlockSpec((1, tm, HW), lambda b, i: (b, i, cols["dk"] // HW)),
                  pl.BlockSpec((1, tm, HW), lambda b, i: (b, i, cols["dv"] // HW)),
                  g, g, tab, tab, mat, mat],
        out_specs=[hd, hd, hd],
        out_shape=[shp, shp, shp],
        compiler_params=_cparams("parallel", "parallel"),
        name="diff_qkv_prep",
    )(proj, proj, proj, gq2, gk2, cosf, sins, perm, seg)


def _softmax_step(s, v, shift, m_scr, l_scr, acc_scr):
    tk = s.shape[1]
    if shift is not None:
        p = jnp.exp2(s - shift)
        part = p[:, 0:LANE]
        for c in range(1, tk // LANE):
            part = part + p[:, c * LANE:(c + 1) * LANE]
        l_scr[...] += part
        acc_scr[...] += jnp.dot(p.astype(BF16), v, preferred_element_type=F32)
        return
    m_prev = m_scr[...]
    m_new = jnp.maximum(m_prev, jnp.max(s, axis=1, keepdims=True))
    alpha = jnp.exp2(m_prev - m_new)
    m_b = jnp.concatenate([m_new] * (tk // LANE), axis=1) if tk % LANE == 0 else m_new[:, 0:1]
    p = jnp.exp2(s - m_b)
    l_scr[...] = alpha * l_scr[...] + jnp.sum(p, axis=1, keepdims=True)
    acc_scr[...] = alpha * acc_scr[...] + jnp.dot(p.astype(BF16), v, preferred_element_type=F32)
    m_scr[...] = m_new


def _row_sum(l_scr, bounded):
    return jnp.sum(l_scr[...], axis=1, keepdims=True) if bounded else l_scr[...]


def _qk(q, k):
    return lax.dot_general(q, k, (((1,), (1,)), ((), ())), preferred_element_type=F32)


def _kv_loop(kv_refs, tk, step, unroll):
    for c in range(len(kv_refs) // 2):
        k_ref, v_ref = kv_refs[2 * c], kv_refs[2 * c + 1]
        nk = k_ref.shape[2]
        t = _tile(nk, tk)

        def body(j, carry, k_ref=k_ref, v_ref=v_ref, t=t):
            r0 = pl.multiple_of(j * t, t)
            step(k_ref[0, 0, pl.ds(r0, t), :], v_ref[0, 0, pl.ds(r0, t), :])
            return carry

        lax.fori_loop(0, nk // t, body, 0, unroll=math.gcd(nk // t, unroll))


def _mla_attn_kernel(*refs, n_kv, tk, bounded):
    shift_ref, q_ref = refs[0], refs[1]
    kv_refs = refs[2:2 + 2 * n_kv]
    o_ref = refs[2 + 2 * n_kv]
    m_scr, l_scr, acc_scr = refs[3 + 2 * n_kv:]
    if not bounded:
        m_scr[...] = jnp.full(m_scr.shape, -jnp.inf, F32)
    l_scr[...] = jnp.zeros(l_scr.shape, F32)
    acc_scr[...] = jnp.zeros(acc_scr.shape, F32)
    q = q_ref[0, 0]
    shift = shift_ref[0] if bounded else None
    _kv_loop(kv_refs, tk, lambda k, v: _softmax_step(_qk(q, k), v, shift, m_scr, l_scr, acc_scr),
             ATTN_UNROLL if bounded else 1)
    o_ref[0] = (acc_scr[...] / _row_sum(l_scr, bounded)).astype(BF16)


def _mla_attn(q, kvs, shift, bounded):
    B, H, N, DQ = q.shape
    tq = _tile(N, 1024)
    in_specs = [pl.BlockSpec(memory_space=pltpu.SMEM),
                pl.BlockSpec((1, 1, tq, DQ), lambda b, h, i: (b, h, i, 0))]
    args = [shift, q]
    for k, v in kvs:
        nk = k.shape[2]
        in_specs += [pl.BlockSpec((1, 1, nk, DQ), lambda b, h, i: (b, h, 0, 0)),
                     pl.BlockSpec((1, 1, nk, MLA_V), lambda b, h, i: (b, h, 0, 0))]
        args += [k, v]
    stat = pltpu.VMEM((tq, LANE), F32)
    return pl.pallas_call(
        functools.partial(_mla_attn_kernel, n_kv=len(kvs), tk=ATTN_TK, bounded=bounded),
        grid=(B, H, N // tq),
        in_specs=in_specs,
        out_specs=pl.BlockSpec((1, tq, MLA_V), lambda b, h, i: (b, i, h)),
        out_shape=jax.ShapeDtypeStruct((B, N, H * MLA_V), BF16),
        scratch_shapes=[stat, stat, stat],
        compiler_params=_cparams("parallel", "parallel", "arbitrary"),
        name="mla_attention",
    )(*args)


def _diff_attn_kernel(*refs, n_kv, tk, lam_init, bounded):
    shift_ref, q_ref = refs[0], refs[1]
    kv_refs = refs[2:2 + 2 * n_kv]
    lamp_ref, gout_ref, o_ref = refs[2 + 2 * n_kv:5 + 2 * n_kv]
    m0, l0, acc0, m1, l1, acc1 = refs[5 + 2 * n_kv:]
    if not bounded:
        for m in (m0, m1):
            m[...] = jnp.full(m.shape, -jnp.inf, F32)
    for z in (l0, l1, acc0, acc1):
        z[...] = jnp.zeros(z.shape, F32)
    q = q_ref[0, 0]
    first = lax.broadcasted_iota(jnp.int32, q.shape, 1) < DIFF_D
    zero = jnp.zeros(q.shape, BF16)
    qa = jnp.where(first, q, zero)
    qb = jnp.where(first, zero, q)
    shift = shift_ref[0] if bounded else None

    def step(k, v):
        _softmax_step(_qk(qa, k), v, shift, m0, l0, acc0)
        _softmax_step(_qk(qb, k), v, shift, m1, l1, acc1)

    _kv_loop(kv_refs, tk, step, ATTN_UNROLL if bounded else 1)

    lp = lamp_ref[...]
    lam = (jnp.exp(jnp.sum(lp[0:1] * lp[1:2], keepdims=True))
           - jnp.exp(jnp.sum(lp[2:3] * lp[3:4], keepdims=True)) + lam_init)
    o = acc0[...] / _row_sum(l0, bounded) - lam * (acc1[...] / _row_sum(l1, bounded))
    on = (o * lax.rsqrt(jnp.mean(o * o, axis=-1, keepdims=True) + EPS)) * gout_ref[...]
    o_ref[0] = (on * (1.0 - lam_init)).astype(BF16)


def _diff_attn(q, kvs, lam_p, g_out, lam_init, shift, bounded):
    B, H, N, DQ = q.shape
    tq = _tile(N, 1024)
    in_specs = [pl.BlockSpec(memory_space=pltpu.SMEM),
                pl.BlockSpec((1, 1, tq, DQ), lambda b, h, i: (b, h, i, 0))]
    args = [shift, q]
    for k, v in kvs:
        nk = k.shape[2]
        kv_spec = pl.BlockSpec((1, 1, nk, LANE), lambda b, h, i: (b, h, 0, 0))
        in_specs += [kv_spec, kv_spec]
        args += [k, v]
    in_specs += [pl.BlockSpec(lam_p.shape, lambda b, h, i: (0, 0)),
                 pl.BlockSpec((1, DIFF_V), lambda b, h, i: (0, 0))]
    args += [lam_p, g_out]
    stat = pltpu.VMEM((tq, LANE), F32)
    return pl.pallas_call(
        functools.partial(_diff_attn_kernel, n_kv=len(kvs), tk=ATTN_TK, lam_init=lam_init, bounded=bounded),
        grid=(B, H, N // tq),
        in_specs=in_specs,
        out_specs=pl.BlockSpec((1, tq, DIFF_V), lambda b, h, i: (b, i, h)),
        out_shape=jax.ShapeDtypeStruct((B, N, H * DIFF_V), BF16),
        scratch_shapes=[stat] * 6,
        compiler_params=_cparams("parallel", "parallel", "arbitrary"),
        name="diff_attention",
    )(*args)


def _merge_kernel(a_ref, b_ref, c_ref, gt_ref, p_ref, o_ref):
    D = o_ref.shape[2]
    acc = None
    for j, br_ref in enumerate((a_ref, b_ref, c_ref)):
        y = _sigmoid(gt_ref[0, :, j * D:(j + 1) * D]) * jnp.dot(br_ref[0], p_ref[j], preferred_element_type=F32)
        acc = y if acc is None else acc + y
    o_ref[0] = acc.astype(BF16)


def _merge(a, b, c, proj, bproj):
    B, N, W = a.shape
    D = bproj.shape[2]
    tm = _tile(N, 256)
    br = pl.BlockSpec((1, tm, W), lambda b_, i: (b_, i, 0))
    return pl.pallas_call(
        _merge_kernel,
        grid=(B, N // tm),
        in_specs=[br, br, br,
                  pl.BlockSpec((1, tm, N_BRANCH * D), lambda b_, i: (b_, i, 0)),
                  pl.BlockSpec((N_BRANCH, W, D), lambda b_, i: (0, 0, 0))],
        out_specs=pl.BlockSpec((1, tm, D), lambda b_, i: (b_, i, 0)),
        out_shape=jax.ShapeDtypeStruct((B, N, D), BF16),
        compiler_params=_cparams("parallel", "parallel"),
        name="branch_merge",
    )(a, b, c, proj, bproj)


def _outproj_kernel(x_ref, y_ref, w_ref, m2_ref, o_ref):
    o_ref[0] = x_ref[0] + m2_ref[0] * jnp.dot(y_ref[0], w_ref[...], preferred_element_type=F32)


def _outproj(x, y, w, m2):
    B, N, D = x.shape
    tm = _tile(N, 512)
    tok = pl.BlockSpec((1, tm, D), lambda b, i: (b, i, 0))
    return pl.pallas_call(
        _outproj_kernel,
        grid=(B, N // tm),
        in_specs=[tok, tok, pl.BlockSpec((D, D), lambda b, i: (0, 0)),
                  pl.BlockSpec((1, 1, D), lambda b, i: (b, 0, 0))],
        out_specs=tok,
        out_shape=jax.ShapeDtypeStruct((B, N, D), F32),
        compiler_params=_cparams("parallel", "parallel"),
        name="mixer_out_proj",
    )(x, y, w, m2)


def _rope_tables(n_tokens, identity):
    half = MLA_ROPE // 2
    if identity:
        return jnp.ones((n_tokens, LANE), F32), jnp.zeros((n_tokens, LANE), F32)
    rows = n_tokens // GRID_W
    axis_dim = MLA_ROPE // 2
    inv_freq = ROPE_THETA ** (-jnp.arange(0, axis_dim, 2, dtype=F32) / axis_dim)
    nf = axis_dim // 2
    row_ang = jnp.arange(rows, dtype=F32)[:, None] * inv_freq
    col_ang = jnp.arange(GRID_W, dtype=F32)[:, None] * inv_freq
    ang = jnp.concatenate([jnp.broadcast_to(row_ang[:, None, :], (rows, GRID_W, nf)),
                           jnp.broadcast_to(col_ang[None, :, :], (rows, GRID_W, nf))], axis=-1)
    ang = ang.reshape(rows * GRID_W, axis_dim)
    cos, sin = jnp.cos(ang), jnp.sin(ang)
    assert half == axis_dim
    return jnp.concatenate([cos, cos, cos, cos], axis=1), jnp.concatenate([-sin, sin, -sin, sin], axis=1)


def _lane_matrices():
    i = jnp.arange(LANE)
    half = MLA_ROPE // 2
    partner = jnp.where(i % MLA_ROPE < half, i + half, i - half)
    perm = (i[:, None] == partner[None, :]).astype(BF16)
    seg = (i[:, None] // DIFF_D == i[None, :] // DIFF_D).astype(BF16)
    return perm, seg


def _in_proj_layout(D, W):
    order = [("gt", N_BRANCH * D), ("rx", W), ("rg", W), ("dq", DIFF_HEADS * 2 * DIFF_D),
             ("dk", DIFF_HEADS * 2 * DIFF_D), ("dv", DIFF_HEADS * DIFF_V)]
    cols, off = {}, 0
    for name, width in order:
        if name != "gt":
            off = -(-off // width) * width
        cols[name] = off
        off += width
    return cols, off


def kernel(x, c, ctx, c_ctx, w_mod, b_mod, norm_g, ffn_w_gate, ffn_w_up, ffn_w_down, w_in, lru_conv_w,
           lru_conv_b, lru_wa, lru_ba, lru_wx, lru_bx, lru_lambda, mla_qc_norm, mla_kvc_norm, mla_w_uq,
           mla_w_ukv, mla_q_norm, mla_k_norm, diff_q_norm, diff_k_norm, diff_lambda, diff_out_norm,
           branch_proj, w_out):
    B, N, D = x.shape
    NC = ctx.shape[1]
    L = w_mod.shape[0]
    W = lru_conv_w.shape[-1]
    QR, KR = mla_qc_norm.shape[-1], mla_kvc_norm.shape[-1]
    dqw = DIFF_HEADS * 2 * DIFF_D

    sizes = dict(rx=W, rg=W, qc=QR, kvc=KR, kr=MLA_ROPE, dq=dqw, dk=dqw, dv=DIFF_HEADS * DIFF_V,
                 gt=N_BRANCH * D)
    src, off = {}, 0
    for name in ("rx", "rg", "qc", "kvc", "kr", "dq", "dk", "dv", "gt"):
        src[name] = off
        off += sizes[name]
    cols, off = _in_proj_layout(D, W)
    for name, width in (("qc", QR), ("kvc", KR), ("kr", LANE)):
        off = -(-off // width) * width
        cols[name] = off
        off += width
    NP = -(-off // 2048) * 2048 if off > 2048 else off

    def pad_w_in(w):
        out = jnp.zeros((D, NP), BF16)
        for name in sizes:
            out = lax.dynamic_update_slice(out, w[:, src[name]:src[name] + sizes[name]].astype(BF16),
                                           (0, cols[name]))
        return out

    def pad_heads(w, width):
        K = w.shape[0]
        w3 = w.reshape(K, MLA_HEADS, width)
        return jnp.pad(w3, ((0, 0), (0, 0), (0, MLA_QK_PAD - width))).reshape(K, MLA_HEADS * MLA_QK_PAD)

    R = -(-(B + 1) // SUBLANE) * SUBLANE
    cpad = jnp.zeros((R, D), F32).at[:B].set(c).at[B].set(c_ctx)
    mod = _compute_mod(cpad, w_mod, b_mod).reshape(L, R, N_MOD, D)

    perm, seg = _lane_matrices()
    tabs_l = _rope_tables(N, identity=False) + (perm,)
    tabs_c = _rope_tables(NC, identity=True) + (perm,)

    xl, xc = x, ctx.reshape(1, B * NC, D)
    for l in range(L):
        last = l == L - 1
        lam_init = 0.8 - 0.6 * math.exp(-0.3 * l)
        mod_l = mod[l, :B]
        mod_c = mod[l, B:B + 1]

        def rows(m, s):
            return m[:, 3 * s:3 * s + 1], m[:, 3 * s + 1:3 * s + 2], m[:, 3 * s + 2:3 * s + 3]

        ng = norm_g[l]
        f1 = (ffn_w_gate[l, 0].astype(BF16), ffn_w_up[l, 0].astype(BF16), ffn_w_down[l, 0].astype(BF16))
        f2 = (ffn_w_gate[l, 1].astype(BF16), ffn_w_up[l, 1].astype(BF16), ffn_w_down[l, 1].astype(BF16))
        w_in_p = pad_w_in(w_in[l])
        wuq = pad_heads(mla_w_uq[l], MLA_QK).astype(BF16)
        wukv = mla_w_ukv[l].astype(BF16)
        gq = jnp.pad(mla_q_norm[l], (0, MLA_QK_PAD - MLA_QK)).reshape(1, MLA_QK_PAD)
        gk = jnp.pad(mla_k_norm[l], (0, MLA_QK_PAD - MLA_QK)).reshape(1, MLA_QK_PAD)
        gqc = mla_qc_norm[l].reshape(1, QR)
        gkvc = mla_kvc_norm[l].reshape(1, KR)
        gdq = jnp.tile(diff_q_norm[l], 2).reshape(1, LANE)
        gdk = jnp.tile(diff_k_norm[l], 2).reshape(1, LANE)
        gdo = diff_out_norm[l].reshape(1, DIFF_V)
        bproj = branch_proj[l].astype(BF16)
        wo = w_out[l].astype(BF16)
        lru = dict(cw=lru_conv_w[l], cb=lru_conv_b[l].reshape(1, W))

        bound_m = math.sqrt(MLA_QK) * jnp.max(jnp.abs(mla_q_norm[l])) * jnp.max(jnp.abs(mla_k_norm[l]))
        bound_d = math.sqrt(DIFF_D) * jnp.max(jnp.abs(diff_q_norm[l])) * jnp.max(jnp.abs(diff_k_norm[l]))

        def mla_attend(q, kvs):
            shift = (bound_m * LOG2E).reshape(1)
            return lax.cond(bound_m <= MAX_SCORE_BOUND,
                            lambda: _mla_attn(q, kvs, shift, True), lambda: _mla_attn(q, kvs, shift, False))

        def diff_attend(q, kvs):
            shift = (bound_d * LOG2E).reshape(1)
            args = (diff_lambda[l], gdo, lam_init, shift)
            return lax.cond(bound_d <= MAX_SCORE_BOUND,
                            lambda: _diff_attn(q, kvs, *args, True), lambda: _diff_attn(q, kvs, *args, False))

        m0, m1, m2 = rows(mod_l, 0)
        xl = _ffn(xl, ng[0:1], m0, m1, m2, *f1)
        m0, m1, m2 = rows(mod_c, 0)
        xc = _ffn(xc, ng[0:1], m0, m1, m2, *f1)

        m0, m1, gate_l = rows(mod_l, 1)
        proj_l = _inproj(xl, ng[1:2], m0, m1, w_in_p)
        m0, m1, gate_c = rows(mod_c, 1)
        proj_c = _inproj(xc, ng[1:2], m0, m1, w_in_p).reshape(B, NC, NP)

        h_zero = jnp.zeros((B, 1, W), F32)
        rx, rg = cols["rx"] // W, cols["rg"] // W
        prm = [dict(wa=lru_wa[l, d].astype(BF16), ba=lru_ba[l, d].reshape(1, W), wx=lru_wx[l, d].astype(BF16),
                    bx=lru_bx[l, d].reshape(1, W), lam=lru_lambda[l, d].reshape(1, W)) for d in range(2)]
        hf_c, xc_c, fin_c = _lru_fwd(proj_c, rx, h_zero, lru["cw"], lru["cb"], W=W, **prm[0])
        hf_l, xc_l, _ = _lru_fwd(proj_l, rx, fin_c, lru["cw"], lru["cb"], W=W, **prm[0])
        a_ctx, fin_c = _lru_rev(xc_c, hf_c, proj_c, rg, h_zero, W=W, **prm[1])
        a_lat, _ = _lru_rev(xc_l, hf_l, proj_l, rg, fin_c, W=W, **prm[1])

        q_ml, k_ml, v_ml = _mla_prep(proj_l, cols, gqc, gkvc, wuq, wukv, gq, gk, *tabs_l)
        q_mc, k_mc, v_mc = _mla_prep(proj_c, cols, gqc, gkvc, wuq, wukv, gq, gk, *tabs_c)
        b_lat = mla_attend(q_ml, [(k_ml, v_ml), (k_mc, v_mc)])

        q_dl, k_dl, v_dl = _diff_prep(proj_l, cols, gdq, gdk, *tabs_l, seg)
        q_dc, k_dc, v_dc = _diff_prep(proj_c, cols, gdq, gdk, *tabs_c, seg)
        c_lat = diff_attend(q_dl, [(k_dl, v_dl), (k_dc, v_dc)])

        y_lat = _merge(a_lat, b_lat, c_lat, proj_l, bproj)
        xl = _outproj(xl, y_lat, wo, gate_l)

        m0, m1, m2 = rows(mod_l, 2)
        xl = _ffn(xl, ng[2:3], m0, m1, m2, *f2)

        if not last:
            b_ctx = mla_attend(q_mc, [(k_mc, v_mc)])
            c_ctx_o = diff_attend(q_dc, [(k_dc, v_dc)])
            y_ctx = _merge(a_ctx, b_ctx, c_ctx_o, proj_c, bproj)
            xc = _outproj(xc, y_ctx.reshape(1, B * NC, D), wo, gate_c)
            m0, m1, m2 = rows(mod_c, 2)
            xc = _ffn(xc, ng[2:3], m0, m1, m2, *f2)
    return xl
```

```python
import functools
import math

import jax
import jax.numpy as jnp
from jax import lax
from jax.experimental import pallas as pl
from jax.experimental.pallas import tpu as pltpu

F32 = jnp.float32
BF16 = jnp.bfloat16

EPS = 1e-6
ROPE_THETA = 10000.0
GRID_W = 64
N_BRANCH = 3
N_MOD = 9
LRU_BLOCKS = 8
CONV_W = 4
LRU_C = 8.0
MLA_HEADS = 8
MLA_NOPE = 128
MLA_ROPE = 64
MLA_V = 128
MLA_QK = MLA_NOPE + MLA_ROPE
DIFF_HEADS = 8
DIFF_D = 64
DIFF_V = 2 * DIFF_D

LANE = 128
SUBLANE = 8
MLA_QK_PAD = 256
VMEM_LIMIT = 56 * 1024 * 1024
FFN_FK = 512
NORM_ROWS = 16
LOG2E = math.log2(math.e)
ATTN_TK = 1024
ATTN_UNROLL = 16
MAX_SCORE_BOUND = 30.0


def _cparams(*sem):
    return pltpu.CompilerParams(dimension_semantics=sem, vmem_limit_bytes=VMEM_LIMIT)


def _tile(n, pref):
    if n <= pref:
        return n
    t = pref - pref % SUBLANE
    while t > SUBLANE and n % t:
        t -= SUBLANE
    assert n % t == 0, (n, pref)
    return t


def _sigmoid(x):
    return 0.5 * jnp.tanh(0.5 * x) + 0.5


def _norm_mod_into(h_scr, x_ref, g, m0, m1):
    tm = h_scr.shape[0]
    rows = NORM_ROWS if tm % NORM_ROWS == 0 else tm
    gm = g * (1.0 + m0)

    def body(c, carry):
        r0 = pl.multiple_of(c * rows, rows)
        x = x_ref[0, pl.ds(r0, rows), :]
        ms = jnp.mean(x * x, axis=-1, keepdims=True)
        h_scr[pl.ds(r0, rows), :] = ((x * lax.rsqrt(ms + EPS)) * gm + m1).astype(BF16)
        return carry

    lax.fori_loop(0, tm // rows, body, 0, unroll=math.gcd(tm // rows, 16))


def _mod_kernel(c_ref, w_ref, b_ref, o_ref):
    c = c_ref[...]
    sc = c * jax.nn.sigmoid(c)
    o_ref[0] = jnp.dot(sc, w_ref[0], preferred_element_type=F32) + b_ref[0]


def _compute_mod(cpad, w_mod, b_mod):
    L, D, ND = w_mod.shape
    R = cpad.shape[0]
    tn = _tile(ND, 1024)
    return pl.pallas_call(
        _mod_kernel,
        grid=(L, ND // tn),
        in_specs=[pl.BlockSpec((R, D), lambda l, j: (0, 0)),
                  pl.BlockSpec((1, D, tn), lambda l, j: (l, 0, j)),
                  pl.BlockSpec((1, 1, tn), lambda l, j: (l, 0, j))],
        out_specs=pl.BlockSpec((1, R, tn), lambda l, j: (l, 0, j)),
        out_shape=jax.ShapeDtypeStruct((L, R, ND), F32),
        compiler_params=_cparams("parallel", "parallel"),
        name="adaln_mod",
    )(cpad, w_mod, b_mod.reshape(L, 1, ND))


def _ffn_kernel(x_ref, g_ref, m0_ref, m1_ref, m2_ref, wg_ref, wu_ref, wd_ref, o_ref, h_scr, *, tail):
    f = pl.program_id(2)
    last = pl.num_programs(2) - 1
    fk = wg_ref.shape[1]

    @pl.when(f == 0)
    def _():
        _norm_mod_into(h_scr, x_ref, g_ref[...], m0_ref[0], m1_ref[0])
        o_ref[0] = jnp.zeros(o_ref.shape[1:], F32)

    def chunk(width):
        h = h_scr[...]
        g = jnp.dot(h, wg_ref[:, 0:width], preferred_element_type=F32)
        u = jnp.dot(h, wu_ref[:, 0:width], preferred_element_type=F32)
        a = (g * jax.nn.sigmoid(g)) * u
        o_ref[0] += jnp.dot(a.astype(BF16), wd_ref[0:width, :], preferred_element_type=F32)

    if tail == fk:
        chunk(fk)
    else:
        pl.when(f < last)(lambda: chunk(fk))
        pl.when(f == last)(lambda: chunk(tail))

    @pl.when(f == last)
    def _():
        o_ref[0] = x_ref[0] + (0.5 * m2_ref[0]) * o_ref[0]


def _ffn(x, g, m0, m1, m2, wg, wu, wd):
    B, N, D = x.shape
    F = wg.shape[1]
    tm = _tile(N, 512)
    fk = min(F, FFN_FK)
    nf = -(-F // fk)
    tail = F - (nf - 1) * fk
    tok = pl.BlockSpec((1, tm, D), lambda b, i, f: (b, i, 0))
    row = pl.BlockSpec((1, 1, D), lambda b, i, f: (b, 0, 0))
    return pl.pallas_call(
        functools.partial(_ffn_kernel, tail=tail),
        grid=(B, N // tm, nf),
        in_specs=[tok, pl.BlockSpec((1, D), lambda b, i, f: (0, 0)), row, row, row,
                  pl.BlockSpec((D, fk), lambda b, i, f: (0, f)),
                  pl.BlockSpec((D, fk), lambda b, i, f: (0, f)),
                  pl.BlockSpec((fk, D), lambda b, i, f: (f, 0))],
        out_specs=tok,
        out_shape=jax.ShapeDtypeStruct((B, N, D), F32),
        scratch_shapes=[pltpu.VMEM((tm, D), BF16)],
        compiler_params=_cparams("parallel", "parallel", "arbitrary"),
        name="swiglu_half_step",
    )(x, g, m0, m1, m2, wg, wu, wd)


def _inproj_kernel(x_ref, g_ref, m0_ref, m1_ref, w_ref, o_ref, h_scr):
    @pl.when(pl.program_id(2) == 0)
    def _():
        _norm_mod_into(h_scr, x_ref, g_ref[...], m0_ref[0], m1_ref[0])

    o_ref[0] = jnp.dot(h_scr[...], w_ref[...], preferred_element_type=F32)


def _inproj(x, g, m0, m1, w):
    B, N, D = x.shape
    NP = w.shape[1]
    tm = _tile(N, 1024)
    tn = _tile(NP, 1024)
    row = pl.BlockSpec((1, 1, D), lambda b, i, j: (b, 0, 0))
    return pl.pallas_call(
        _inproj_kernel,
        grid=(B, N // tm, NP // tn),
        in_specs=[pl.BlockSpec((1, tm, D), lambda b, i, j: (b, i, 0)),
                  pl.BlockSpec((1, D), lambda b, i, j: (0, 0)), row, row,
                  pl.BlockSpec((D, tn), lambda b, i, j: (0, j))],
        out_specs=pl.BlockSpec((1, tm, tn), lambda b, i, j: (b, i, j)),
        out_shape=jax.ShapeDtypeStruct((B, N, NP), F32),
        scratch_shapes=[pltpu.VMEM((tm, D), BF16)],
        compiler_params=_cparams("parallel", "parallel", "arbitrary"),
        name="mixer_in_proj",
    )(x, g, m0, m1, w)


def _lru_gates_and_scan(xc_of, wa_ref, ba_ref, wx_ref, bx_ref, lam_ref, h0_ref, store_h, hfin_ref,
                        a_scr, u_scr, carry_scr, *, reverse, tm, W):
    H = SUBLANE

    @pl.when(pl.program_id(1) == 0)
    def _():
        carry_scr[...] = h0_ref[0]

    sp = jax.nn.softplus(-lam_ref[...])
    lb = W // LRU_BLOCKS
    for n in range(LRU_BLOCKS):
        sl = slice(n * lb, (n + 1) * lb)
        xc = xc_of(sl)
        xb = xc.astype(BF16)
        r = _sigmoid(jnp.dot(xb, wa_ref[n], preferred_element_type=F32) + ba_ref[:, sl])
        ig = _sigmoid(jnp.dot(xb, wx_ref[n], preferred_element_type=F32) + bx_ref[:, sl])
        a = jnp.exp((-LRU_C) * r * sp[:, sl])
        a_scr[:, sl] = a
        u_scr[:, sl] = jnp.sqrt(1.0 - a * a) * (ig * xc)

    rowi = lax.broadcasted_iota(jnp.int32, (H, W), 0)
    ng = tm // H

    def group(gi, h):
        g = (ng - 1 - gi) if reverse else gi
        r0 = pl.multiple_of(g * H, H)
        a = a_scr[pl.ds(r0, H), :]
        u = u_scr[pl.ds(r0, H), :]
        for s in (1, 2, 4):
            sh = (H - s) if reverse else s
            valid = (rowi < H - s) if reverse else (rowi >= s)
            a_s = pltpu.roll(a, sh, 0)
            u_s = pltpu.roll(u, sh, 0)
            u = jnp.where(valid, a * u_s + u, u)
            a = jnp.where(valid, a * a_s, a)
        hh = a * h + u
        store_h(r0, hh)
        return hh[0:1, :] if reverse else hh[H - 1:H, :]

    h_last = lax.fori_loop(0, ng, group, carry_scr[...])
    carry_scr[...] = h_last
    hfin_ref[0] = h_last


def _lru_fwd_kernel(x_ref, xp_ref, xn_ref, cw_ref, cb_ref, wa_ref, ba_ref, wx_ref, bx_ref, lam_ref, h0_ref,
                    h_ref, xc_ref, hfin_ref, xs_scr, a_scr, u_scr, carry_scr, *, nt, tm, W):
    ti = pl.program_id(1)
    H = SUBLANE
    zero = jnp.zeros((H, W), F32)
    xs_scr[0:H, :] = jnp.where(ti > 0, xp_ref[0], zero)
    xs_scr[H:H + tm, :] = x_ref[0]
    xs_scr[H + tm:2 * H + tm, :] = jnp.where(ti < nt - 1, xn_ref[0], zero)
    left = CONV_W // 2

    def conv(sl):
        xc = cb_ref[:, sl] + xs_scr[H - left:H - left + tm, sl] * cw_ref[0:1, sl]
        for j in range(1, CONV_W):
            xc = xc + xs_scr[H - left + j:H - left + j + tm, sl] * cw_ref[j:j + 1, sl]
        xc_ref[0, :, sl] = xc
        return xc

    def store_h(r0, hh):
        h_ref[0, pl.ds(r0, H), :] = hh

    _lru_gates_and_scan(conv, wa_ref, ba_ref, wx_ref, bx_ref, lam_ref, h0_ref, store_h, hfin_ref,
                        a_scr, u_scr, carry_scr, reverse=False, tm=tm, W=W)


def _lru_rev_kernel(xc_ref, hf_ref, rg_ref, wa_ref, ba_ref, wx_ref, bx_ref, lam_ref, h0_ref,
                    y_ref, hfin_ref, hb_scr, a_scr, u_scr, carry_scr, *, tm, W):
    def store_h(r0, hh):
        hb_scr[pl.ds(r0, SUBLANE), :] = hh

    _lru_gates_and_scan(lambda sl: xc_ref[0, :, sl], wa_ref, ba_ref, wx_ref, bx_ref, lam_ref, h0_ref,
                        store_h, hfin_ref, a_scr, u_scr, carry_scr, reverse=True, tm=tm, W=W)
    lb = W // LRU_BLOCKS
    for n in range(LRU_BLOCKS):
        sl = slice(n * lb, (n + 1) * lb)
        y_ref[0, :, sl] = ((hf_ref[0, :, sl] + hb_scr[:, sl]) * jax.nn.gelu(rg_ref[0, :, sl])).astype(BF16)


def _lru_specs(W):
    vec = pl.BlockSpec((1, W), lambda b, i: (0, 0))
    blk = pl.BlockSpec((LRU_BLOCKS, W // LRU_BLOCKS, W // LRU_BLOCKS), lambda b, i: (0, 0, 0))
    state = pl.BlockSpec((1, 1, W), lambda b, i: (b, 0, 0))
    return vec, blk, state


def _lru_fwd(proj, col_blk, h0, cw, cb, wa, ba, wx, bx, lam, *, W):
    B, N, _ = proj.shape
    tm = _tile(N, 256)
    nt = N // tm
    r8 = tm // SUBLANE
    nb8 = N // SUBLANE
    vec, blk, state = _lru_specs(W)
    tok = pl.BlockSpec((1, tm, W), lambda b, i: (b, i, 0))
    full = jax.ShapeDtypeStruct((B, N, W), F32)
    return pl.pallas_call(
        functools.partial(_lru_fwd_kernel, nt=nt, tm=tm, W=W),
        grid=(B, nt),
        in_specs=[pl.BlockSpec((1, tm, W), lambda b, i: (b, i, col_blk)),
                  pl.BlockSpec((1, SUBLANE, W), lambda b, i: (b, jnp.maximum(i * r8 - 1, 0), col_blk)),
                  pl.BlockSpec((1, SUBLANE, W), lambda b, i: (b, jnp.minimum((i + 1) * r8, nb8 - 1), col_blk)),
                  pl.BlockSpec((CONV_W, W), lambda b, i: (0, 0)), vec, blk, vec, blk, vec, vec, state],
        out_specs=[tok, tok, state],
        out_shape=[full, full, jax.ShapeDtypeStruct((B, 1, W), F32)],
        scratch_shapes=[pltpu.VMEM((tm + 2 * SUBLANE, W), F32), pltpu.VMEM((tm, W), F32),
                        pltpu.VMEM((tm, W), F32), pltpu.VMEM((1, W), F32)],
        compiler_params=_cparams("parallel", "arbitrary"),
        name="rglru_fwd",
    )(proj, proj, proj, cw, cb, wa, ba, wx, bx, lam, h0)


def _lru_rev(xc, hf, proj, gate_blk, h0, wa, ba, wx, bx, lam, *, W):
    B, N, _ = xc.shape
    tm = _tile(N, 256)
    nt = N // tm
    vec, blk, state = _lru_specs(W)
    tok = pl.BlockSpec((1, tm, W), lambda b, i: (b, nt - 1 - i, 0))
    return pl.pallas_call(
        functools.partial(_lru_rev_kernel, tm=tm, W=W),
        grid=(B, nt),
        in_specs=[tok, tok, pl.BlockSpec((1, tm, W), lambda b, i: (b, nt - 1 - i, gate_blk)),
                  blk, vec, blk, vec, vec, state],
        out_specs=[tok, state],
        out_shape=[jax.ShapeDtypeStruct((B, N, W), BF16), jax.ShapeDtypeStruct((B, 1, W), F32)],
        scratch_shapes=[pltpu.VMEM((tm, W), F32), pltpu.VMEM((tm, W), F32),
                        pltpu.VMEM((tm, W), F32), pltpu.VMEM((1, W), F32)],
        compiler_params=_cparams("parallel", "arbitrary"),
        name="rglru_rev_gate",
    )(xc, hf, proj, wa, ba, wx, bx, lam, h0)


def _rope(x, cosf, sins, perm):
    return x * cosf + jnp.dot(x.astype(BF16), perm, preferred_element_type=F32) * sins


def _mla_prep_kernel(qc_ref, kvc_ref, kr_ref, gqc_ref, gkvc_ref, wuq_ref, wukv_ref, gq_ref, gk_ref,
                     cos_ref, sin_ref, perm_ref, q_ref, k_ref, v_ref):
    cosf, sins, perm = cos_ref[...], sin_ref[...], perm_ref[...]
    gqs = gq_ref[...] * (MLA_QK ** -0.5 * LOG2E)

    qc = qc_ref[0]
    qn = (qc * lax.rsqrt(jnp.mean(qc * qc, axis=-1, keepdims=True) + EPS)) * gqc_ref[...]
    qf = jnp.dot(qn.astype(BF16), wuq_ref[...], preferred_element_type=F32)
    kvc = kvc_ref[0]
    kvn = (kvc * lax.rsqrt(jnp.mean(kvc * kvc, axis=-1, keepdims=True) + EPS)) * gkvc_ref[...]
    kvf = jnp.dot(kvn.astype(BF16), wukv_ref[...], preferred_element_type=F32)

    kr = kr_ref[0]
    ss_kr = jnp.sum(kr * kr, axis=-1, keepdims=True)
    kr_rot = _rope(kr * gk_ref[:, LANE:], cosf, sins, perm)

    for h in range(MLA_HEADS):
        o = h * MLA_QK_PAD
        qa = qf[:, o:o + LANE]
        qb = qf[:, o + LANE:o + 2 * LANE]
        ss = jnp.sum(qa * qa + qb * qb, axis=-1, keepdims=True)
        rinv = lax.rsqrt(ss * (1.0 / MLA_QK) + EPS)
        q_ref[0, h, :, 0:LANE] = ((qa * rinv) * gqs[:, 0:LANE]).astype(BF16)
        q_ref[0, h, :, LANE:] = _rope((qb * rinv) * gqs[:, LANE:], cosf, sins, perm).astype(BF16)

        kn = kvf[:, o:o + LANE]
        ssk = jnp.sum(kn * kn, axis=-1, keepdims=True) + ss_kr
        rk = lax.rsqrt(ssk * (1.0 / MLA_QK) + EPS)
        k_ref[0, h, :, 0:LANE] = ((kn * rk) * gk_ref[:, 0:LANE]).astype(BF16)
        k_ref[0, h, :, LANE:] = (kr_rot * rk).astype(BF16)
        v_ref[0, h] = kvf[:, o + LANE:o + 2 * LANE].astype(BF16)


def _mla_prep(proj, cols, gqc, gkvc, wuq, wukv, gq, gk, cosf, sins, perm):
    B, N, _ = proj.shape
    tm = _tile(N, 256)
    QR, KR = gqc.shape[1], gkvc.shape[1]
    HP = MLA_HEADS * MLA_QK_PAD
    tab = pl.BlockSpec((tm, LANE), lambda b, i: (i, 0))

    def full(a):
        return pl.BlockSpec(a.shape, lambda b, i: (0,) * a.ndim)

    hd = lambda w: pl.BlockSpec((1, MLA_HEADS, tm, w), lambda b, i: (b, 0, i, 0))
    return pl.pallas_call(
        _mla_prep_kernel,
        grid=(B, N // tm),
        in_specs=[pl.BlockSpec((1, tm, QR), lambda b, i: (b, i, cols["qc"] // QR)),
                  pl.BlockSpec((1, tm, KR), lambda b, i: (b, i, cols["kvc"] // KR)),
                  pl.BlockSpec((1, tm, LANE), lambda b, i: (b, i, cols["kr"] // LANE)),
                  full(gqc), full(gkvc), full(wuq), full(wukv), full(gq), full(gk), tab, tab, full(perm)],
        out_specs=[hd(MLA_QK_PAD), hd(MLA_QK_PAD), hd(MLA_V)],
        out_shape=[jax.ShapeDtypeStruct((B, MLA_HEADS, N, MLA_QK_PAD), BF16),
                   jax.ShapeDtypeStruct((B, MLA_HEADS, N, MLA_QK_PAD), BF16),
                   jax.ShapeDtypeStruct((B, MLA_HEADS, N, MLA_V), BF16)],
        compiler_params=_cparams("parallel", "parallel"),
        name="mla_qkv_prep",
    )(proj, proj, proj, gqc, gkvc, wuq, wukv, gq, gk, cosf, sins, perm)


def _diff_prep_kernel(dq_ref, dk_ref, dv_ref, gq_ref, gk_ref, cos_ref, sin_ref, perm_ref, seg_ref,
                      q_ref, k_ref, v_ref):
    cosf, sins, perm, seg = cos_ref[...], sin_ref[...], perm_ref[...], seg_ref[...]
    gqs = gq_ref[...] * (DIFF_D ** -0.5 * LOG2E)
    gk = gk_ref[...]

    def norm_rope(x, g):
        ss = jnp.dot((x * x).astype(BF16), seg, preferred_element_type=F32)
        return _rope((x * lax.rsqrt(ss * (1.0 / DIFF_D) + EPS)) * g, cosf, sins, perm)

    for h in range(DIFF_HEADS):
        sl = slice(h * LANE, (h + 1) * LANE)
        q_ref[0, h] = norm_rope(dq_ref[0, :, sl], gqs).astype(BF16)
        k_ref[0, h] = norm_rope(dk_ref[0, :, sl], gk).astype(BF16)
        v_ref[0, h] = dv_ref[0, :, sl].astype(BF16)


def _diff_prep(proj, cols, gq2, gk2, cosf, sins, perm, seg):
    B, N, _ = proj.shape
    tm = _tile(N, 256)
    HW = DIFF_HEADS * LANE
    tab = pl.BlockSpec((tm, LANE), lambda b, i: (i, 0))
    mat = pl.BlockSpec((LANE, LANE), lambda b, i: (0, 0))
    g = pl.BlockSpec((1, LANE), lambda b, i: (0, 0))
    hd = pl.BlockSpec((1, DIFF_HEADS, tm, LANE), lambda b, i: (b, 0, i, 0))
    shp = jax.ShapeDtypeStruct((B, DIFF_HEADS, N, LANE), BF16)
    return pl.pallas_call(
        _diff_prep_kernel,
        grid=(B, N // tm),
        in_specs=[pl.BlockSpec((1, tm, HW), lambda b, i: (b, i, cols["dq"] // HW)),
                  pl.BlockSpec((1, tm, HW), lambda b, i: (b, i, cols["dk"] // HW)),
                  pl.BlockSpec((1, tm, HW), lambda b, i: (b, i, cols["dv"] // HW)),
                  g, g, tab, tab, mat, mat],
        out_specs=[hd, hd, hd],
        out_shape=[shp, shp, shp],
        compiler_params=_cparams("parallel", "parallel"),
        name="diff_qkv_prep",
    )(proj, proj, proj, gq2, gk2, cosf, sins, perm, seg)


def _softmax_step(s, v, shift, m_scr, l_scr, acc_scr):
    tk = s.shape[1]
    if shift is not None:
        p = jnp.exp2(s - shift)
        part = p[:, 0:LANE]
        for c in range(1, tk // LANE):
            part = part + p[:, c * LANE:(c + 1) * LANE]
        l_scr[...] += part
        acc_scr[...] += jnp.dot(p.astype(BF16), v, preferred_element_type=F32)
        return
    m_prev = m_scr[...]
    m_new = jnp.maximum(m_prev, jnp.max(s, axis=1, keepdims=True))
    alpha = jnp.exp2(m_prev - m_new)
    m_b = jnp.concatenate([m_new] * (tk // LANE), axis=1) if tk % LANE == 0 else m_new[:, 0:1]
    p = jnp.exp2(s - m_b)
    l_scr[...] = alpha * l_scr[...] + jnp.sum(p, axis=1, keepdims=True)
    acc_scr[...] = alpha * acc_scr[...] + jnp.dot(p.astype(BF16), v, preferred_element_type=F32)
    m_scr[...] = m_new


def _row_sum(l_scr, bounded):
    return jnp.sum(l_scr[...], axis=1, keepdims=True) if bounded else l_scr[...]


def _qk(q, k):
    return lax.dot_general(q, k, (((1,), (1,)), ((), ())), preferred_element_type=F32)


def _kv_loop(kv_refs, tk, step, unroll):
    for c in range(len(kv_refs) // 2):
        k_ref, v_ref = kv_refs[2 * c], kv_refs[2 * c + 1]
        nk = k_ref.shape[2]
        t = _tile(nk, tk)

        def body(j, carry, k_ref=k_ref, v_ref=v_ref, t=t):
            r0 = pl.multiple_of(j * t, t)
            step(k_ref[0, 0, pl.ds(r0, t), :], v_ref[0, 0, pl.ds(r0, t), :])
            return carry

        lax.fori_loop(0, nk // t, body, 0, unroll=math.gcd(nk // t, unroll))


def _mla_attn_kernel(*refs, n_kv, tk, bounded):
    shift_ref, q_ref = refs[0], refs[1]
    kv_refs = refs[2:2 + 2 * n_kv]
    o_ref = refs[2 + 2 * n_kv]
    m_scr, l_scr, acc_scr = refs[3 + 2 * n_kv:]
    if not bounded:
        m_scr[...] = jnp.full(m_scr.shape, -jnp.inf, F32)
    l_scr[...] = jnp.zeros(l_scr.shape, F32)
    acc_scr[...] = jnp.zeros(acc_scr.shape, F32)
    q = q_ref[0, 0]
    shift = shift_ref[0] if bounded else None
    _kv_loop(kv_refs, tk, lambda k, v: _softmax_step(_qk(q, k), v, shift, m_scr, l_scr, acc_scr),
             ATTN_UNROLL if bounded else 1)
    o_ref[0] = (acc_scr[...] / _row_sum(l_scr, bounded)).astype(BF16)


def _mla_attn(q, kvs, shift, bounded):
    B, H, N, DQ = q.shape
    tq = _tile(N, 1024)
    in_specs = [pl.BlockSpec(memory_space=pltpu.SMEM),
                pl.BlockSpec((1, 1, tq, DQ), lambda b, h, i: (b, h, i, 0))]
    args = [shift, q]
    for k, v in kvs:
        nk = k.shape[2]
        in_specs += [pl.BlockSpec((1, 1, nk, DQ), lambda b, h, i: (b, h, 0, 0)),
                     pl.BlockSpec((1, 1, nk, MLA_V), lambda b, h, i: (b, h, 0, 0))]
        args += [k, v]
    stat = pltpu.VMEM((tq, LANE), F32)
    return pl.pallas_call(
        functools.partial(_mla_attn_kernel, n_kv=len(kvs), tk=ATTN_TK, bounded=bounded),
        grid=(B, H, N // tq),
        in_specs=in_specs,
        out_specs=pl.BlockSpec((1, tq, MLA_V), lambda b, h, i: (b, i, h)),
        out_shape=jax.ShapeDtypeStruct((B, N, H * MLA_V), BF16),
        scratch_shapes=[stat, stat, stat],
        compiler_params=_cparams("parallel", "parallel", "arbitrary"),
        name="mla_attention",
    )(*args)


def _diff_attn_kernel(*refs, n_kv, tk, lam_init, bounded):
    shift_ref, q_ref = refs[0], refs[1]
    kv_refs = refs[2:2 + 2 * n_kv]
    lamp_ref, gout_ref, o_ref = refs[2 + 2 * n_kv:5 + 2 * n_kv]
    m0, l0, acc0, m1, l1, acc1 = refs[5 + 2 * n_kv:]
    if not bounded:
        for m in (m0, m1):
            m[...] = jnp.full(m.shape, -jnp.inf, F32)
    for z in (l0, l1, acc0, acc1):
        z[...] = jnp.zeros(z.shape, F32)
    q = q_ref[0, 0]
    first = lax.broadcasted_iota(jnp.int32, q.shape, 1) < DIFF_D
    zero = jnp.zeros(q.shape, BF16)
    qa = jnp.where(first, q, zero)
    qb = jnp.where(first, zero, q)
    shift = shift_ref[0] if bounded else None

    def step(k, v):
        _softmax_step(_qk(qa, k), v, shift, m0, l0, acc0)
        _softmax_step(_qk(qb, k), v, shift, m1, l1, acc1)

    _kv_loop(kv_refs, tk, step, ATTN_UNROLL if bounded else 1)

    lp = lamp_ref[...]
    lam = (jnp.exp(jnp.sum(lp[0:1] * lp[1:2], keepdims=True))
           - jnp.exp(jnp.sum(lp[2:3] * lp[3:4], keepdims=True)) + lam_init)
    o = acc0[...] / _row_sum(l0, bounded) - lam * (acc1[...] / _row_sum(l1, bounded))
    on = (o * lax.rsqrt(jnp.mean(o * o, axis=-1, keepdims=True) + EPS)) * gout_ref[...]
    o_ref[0] = (on * (1.0 - lam_init)).astype(BF16)


def _diff_attn(q, kvs, lam_p, g_out, lam_init, shift, bounded):
    B, H, N, DQ = q.shape
    tq = _tile(N, 1024)
    in_specs = [pl.BlockSpec(memory_space=pltpu.SMEM),
                pl.BlockSpec((1, 1, tq, DQ), lambda b, h, i: (b, h, i, 0))]
    args = [shift, q]
    for k, v in kvs:
        nk = k.shape[2]
        kv_spec = pl.BlockSpec((1, 1, nk, LANE), lambda b, h, i: (b, h, 0, 0))
        in_specs += [kv_spec, kv_spec]
        args += [k, v]
    in_specs += [pl.BlockSpec(lam_p.shape, lambda b, h, i: (0, 0)),
                 pl.BlockSpec((1, DIFF_V), lambda b, h, i: (0, 0))]
    args += [lam_p, g_out]
    stat = pltpu.VMEM((tq, LANE), F32)
    return pl.pallas_call(
        functools.partial(_diff_attn_kernel, n_kv=len(kvs), tk=ATTN_TK, lam_init=lam_init, bounded=bounded),
        grid=(B, H, N // tq),
        in_specs=in_specs,
        out_specs=pl.BlockSpec((1, tq, DIFF_V), lambda b, h, i: (b, i, h)),
        out_shape=jax.ShapeDtypeStruct((B, N, H * DIFF_V), BF16),
        scratch_shapes=[stat] * 6,
        compiler_params=_cparams("parallel", "parallel", "arbitrary"),
        name="diff_attention",
    )(*args)


def _merge_kernel(a_ref, b_ref, c_ref, gt_ref, p_ref, o_ref):
    D = o_ref.shape[2]
    acc = None
    for j, br_ref in enumerate((a_ref, b_ref, c_ref)):
        y = _sigmoid(gt_ref[0, :, j * D:(j + 1) * D]) * jnp.dot(br_ref[0], p_ref[j], preferred_element_type=F32)
        acc = y if acc is None else acc + y
    o_ref[0] = acc.astype(BF16)


def _merge(a, b, c, proj, bproj):
    B, N, W = a.shape
    D = bproj.shape[2]
    tm = _tile(N, 256)
    br = pl.BlockSpec((1, tm, W), lambda b_, i: (b_, i, 0))
    return pl.pallas_call(
        _merge_kernel,
        grid=(B, N // tm),
        in_specs=[br, br, br,
                  pl.BlockSpec((1, tm, N_BRANCH * D), lambda b_, i: (b_, i, 0)),
                  pl.BlockSpec((N_BRANCH, W, D), lambda b_, i: (0, 0, 0))],
        out_specs=pl.BlockSpec((1, tm, D), lambda b_, i: (b_, i, 0)),
        out_shape=jax.ShapeDtypeStruct((B, N, D), BF16),
        compiler_params=_cparams("parallel", "parallel"),
        name="branch_merge",
    )(a, b, c, proj, bproj)


def _outproj_kernel(x_ref, y_ref, w_ref, m2_ref, o_ref):
    o_ref[0] = x_ref[0] + m2_ref[0] * jnp.dot(y_ref[0], w_ref[...], preferred_element_type=F32)


def _outproj(x, y, w, m2):
    B, N, D = x.shape
    tm = _tile(N, 512)
    tok = pl.BlockSpec((1, tm, D), lambda b, i: (b, i, 0))
    return pl.pallas_call(
        _outproj_kernel,
        grid=(B, N // tm),
        in_specs=[tok, tok, pl.BlockSpec((D, D), lambda b, i: (0, 0)),
                  pl.BlockSpec((1, 1, D), lambda b, i: (b, 0, 0))],
        out_specs=tok,
        out_shape=jax.ShapeDtypeStruct((B, N, D), F32),
        compiler_params=_cparams("parallel", "parallel"),
        name="mixer_out_proj",
    )(x, y, w, m2)


def _rope_tables(n_tokens, identity):
    half = MLA_ROPE // 2
    if identity:
        return jnp.ones((n_tokens, LANE), F32), jnp.zeros((n_tokens, LANE), F32)
    rows = n_tokens // GRID_W
    axis_dim = MLA_ROPE // 2
    inv_freq = ROPE_THETA ** (-jnp.arange(0, axis_dim, 2, dtype=F32) / axis_dim)
    nf = axis_dim // 2
    row_ang = jnp.arange(rows, dtype=F32)[:, None] * inv_freq
    col_ang = jnp.arange(GRID_W, dtype=F32)[:, None] * inv_freq
    ang = jnp.concatenate([jnp.broadcast_to(row_ang[:, None, :], (rows, GRID_W, nf)),
                           jnp.broadcast_to(col_ang[None, :, :], (rows, GRID_W, nf))], axis=-1)
    ang = ang.reshape(rows * GRID_W, axis_dim)
    cos, sin = jnp.cos(ang), jnp.sin(ang)
    assert half == axis_dim
    return jnp.concatenate([cos, cos, cos, cos], axis=1), jnp.concatenate([-sin, sin, -sin, sin], axis=1)


def _lane_matrices():
    i = jnp.arange(LANE)
    half = MLA_ROPE // 2
    partner = jnp.where(i % MLA_ROPE < half, i + half, i - half)
    perm = (i[:, None] == partner[None, :]).astype(BF16)
    seg = (i[:, None] // DIFF_D == i[None, :] // DIFF_D).astype(BF16)
    return perm, seg


def _in_proj_layout(D, W):
    order = [("gt", N_BRANCH * D), ("rx", W), ("rg", W), ("dq", DIFF_HEADS * 2 * DIFF_D),
             ("dk", DIFF_HEADS * 2 * DIFF_D), ("dv", DIFF_HEADS * DIFF_V)]
    cols, off = {}, 0
    for name, width in order:
        if name != "gt":
            off = -(-off // width) * width
        cols[name] = off
        off += width
    return cols, off


def kernel(x, c, ctx, c_ctx, w_mod, b_mod, norm_g, ffn_w_gate, ffn_w_up, ffn_w_down, w_in, lru_conv_w,
           lru_conv_b, lru_wa, lru_ba, lru_wx, lru_bx, lru_lambda, mla_qc_norm, mla_kvc_norm, mla_w_uq,
           mla_w_ukv, mla_q_norm, mla_k_norm, diff_q_norm, diff_k_norm, diff_lambda, diff_out_norm,
           branch_proj, w_out):
    B, N, D = x.shape
    NC = ctx.shape[1]
    L = w_mod.shape[0]
    W = lru_conv_w.shape[-1]
    QR, KR = mla_qc_norm.shape[-1], mla_kvc_norm.shape[-1]
    dqw = DIFF_HEADS * 2 * DIFF_D

    sizes = dict(rx=W, rg=W, qc=QR, kvc=KR, kr=MLA_ROPE, dq=dqw, dk=dqw, dv=DIFF_HEADS * DIFF_V,
                 gt=N_BRANCH * D)
    src, off = {}, 0
    for name in ("rx", "rg", "qc", "kvc", "kr", "dq", "dk", "dv", "gt"):
        src[name] = off
        off += sizes[name]
    cols, off = _in_proj_layout(D, W)
    for name, width in (("qc", QR), ("kvc", KR), ("kr", LANE)):
        off = -(-off // width) * width
        cols[name] = off
        off += width
    NP = -(-off // 2048) * 2048 if off > 2048 else off

    def pad_w_in(w):
        out = jnp.zeros((D, NP), BF16)
        for name in sizes:
            out = lax.dynamic_update_slice(out, w[:, src[name]:src[name] + sizes[name]].astype(BF16),
                                           (0, cols[name]))
        return out

    def pad_heads(w, width):
        K = w.shape[0]
        w3 = w.reshape(K, MLA_HEADS, width)
        return jnp.pad(w3, ((0, 0), (0, 0), (0, MLA_QK_PAD - width))).reshape(K, MLA_HEADS * MLA_QK_PAD)

    R = -(-(B + 1) // SUBLANE) * SUBLANE
    cpad = jnp.zeros((R, D), F32).at[:B].set(c).at[B].set(c_ctx)
    mod = _compute_mod(cpad, w_mod, b_mod).reshape(L, R, N_MOD, D)

    perm, seg = _lane_matrices()
    tabs_l = _rope_tables(N, identity=False) + (perm,)
    tabs_c = _rope_tables(NC, identity=True) + (perm,)

    xl, xc = x, ctx.reshape(1, B * NC, D)
    for l in range(L):
        last = l == L - 1
        lam_init = 0.8 - 0.6 * math.exp(-0.3 * l)
        mod_l = mod[l, :B]
        mod_c = mod[l, B:B + 1]

        def rows(m, s):
            return m[:, 3 * s:3 * s + 1], m[:, 3 * s + 1:3 * s + 2], m[:, 3 * s + 2:3 * s + 3]

        ng = norm_g[l]
        f1 = (ffn_w_gate[l, 0].astype(BF16), ffn_w_up[l, 0].astype(BF16), ffn_w_down[l, 0].astype(BF16))
        f2 = (ffn_w_gate[l, 1].astype(BF16), ffn_w_up[l, 1].astype(BF16), ffn_w_down[l, 1].astype(BF16))
        w_in_p = pad_w_in(w_in[l])
        wuq = pad_heads(mla_w_uq[l], MLA_QK).astype(BF16)
        wukv = mla_w_ukv[l].astype(BF16)
        gq = jnp.pad(mla_q_norm[l], (0, MLA_QK_PAD - MLA_QK)).reshape(1, MLA_QK_PAD)
        gk = jnp.pad(mla_k_norm[l], (0, MLA_QK_PAD - MLA_QK)).reshape(1, MLA_QK_PAD)
        gqc = mla_qc_norm[l].reshape(1, QR)
        gkvc = mla_kvc_norm[l].reshape(1, KR)
        gdq = jnp.tile(diff_q_norm[l], 2).reshape(1, LANE)
        gdk = jnp.tile(diff_k_norm[l], 2).reshape(1, LANE)
        gdo = diff_out_norm[l].reshape(1, DIFF_V)
        bproj = branch_proj[l].astype(BF16)
        wo = w_out[l].astype(BF16)
        lru = dict(cw=lru_conv_w[l], cb=lru_conv_b[l].reshape(1, W))

        bound_m = math.sqrt(MLA_QK) * jnp.max(jnp.abs(mla_q_norm[l])) * jnp.max(jnp.abs(mla_k_norm[l]))
        bound_d = math.sqrt(DIFF_D) * jnp.max(jnp.abs(diff_q_norm[l])) * jnp.max(jnp.abs(diff_k_norm[l]))

        def mla_attend(q, kvs):
            shift = (bound_m * LOG2E).reshape(1)
            return lax.cond(bound_m <= MAX_SCORE_BOUND,
                            lambda: _mla_attn(q, kvs, shift, True), lambda: _mla_attn(q, kvs, shift, False))

        def diff_attend(q, kvs):
            shift = (bound_d * LOG2E).reshape(1)
            args = (diff_lambda[l], gdo, lam_init, shift)
            return lax.cond(bound_d <= MAX_SCORE_BOUND,
                            lambda: _diff_attn(q, kvs, *args, True), lambda: _diff_attn(q, kvs, *args, False))

        m0, m1, m2 = rows(mod_l, 0)
        xl = _ffn(xl, ng[0:1], m0, m1, m2, *f1)
        m0, m1, m2 = rows(mod_c, 0)
        xc = _ffn(xc, ng[0:1], m0, m1, m2, *f1)

        m0, m1, gate_l = rows(mod_l, 1)
        proj_l = _inproj(xl, ng[1:2], m0, m1, w_in_p)
        m0, m1, gate_c = rows(mod_c, 1)
        proj_c = _inproj(xc, ng[1:2], m0, m1, w_in_p).reshape(B, NC, NP)

        h_zero = jnp.zeros((B, 1, W), F32)
        rx, rg = cols["rx"] // W, cols["rg"] // W
        prm = [dict(wa=lru_wa[l, d].astype(BF16), ba=lru_ba[l, d].reshape(1, W), wx=lru_wx[l, d].astype(BF16),
                    bx=lru_bx[l, d].reshape(1, W), lam=lru_lambda[l, d].reshape(1, W)) for d in range(2)]
        hf_c, xc_c, fin_c = _lru_fwd(proj_c, rx, h_zero, lru["cw"], lru["cb"], W=W, **prm[0])
        hf_l, xc_l, _ = _lru_fwd(proj_l, rx, fin_c, lru["cw"], lru["cb"], W=W, **prm[0])
        a_ctx, fin_c = _lru_rev(xc_c, hf_c, proj_c, rg, h_zero, W=W, **prm[1])
        a_lat, _ = _lru_rev(xc_l, hf_l, proj_l, rg, fin_c, W=W, **prm[1])

        q_ml, k_ml, v_ml = _mla_prep(proj_l, cols, gqc, gkvc, wuq, wukv, gq, gk, *tabs_l)
        q_mc, k_mc, v_mc = _mla_prep(proj_c, cols, gqc, gkvc, wuq, wukv, gq, gk, *tabs_c)
        b_lat = mla_attend(q_ml, [(k_ml, v_ml), (k_mc, v_mc)])

        q_dl, k_dl, v_dl = _diff_prep(proj_l, cols, gdq, gdk, *tabs_l, seg)
        q_dc, k_dc, v_dc = _diff_prep(proj_c, cols, gdq, gdk, *tabs_c, seg)
        c_lat = diff_attend(q_dl, [(k_dl, v_dl), (k_dc, v_dc)])

        y_lat = _merge(a_lat, b_lat, c_lat, proj_l, bproj)
        xl = _outproj(xl, y_lat, wo, gate_l)

        m0, m1, m2 = rows(mod_l, 2)
        xl = _ffn(xl, ng[2:3], m0, m1, m2, *f2)

        if not last:
            b_ctx = mla_attend(q_mc, [(k_mc, v_mc)])
            c_ctx_o = diff_attend(q_dc, [(k_dc, v_dc)])
            y_ctx = _merge(a_ctx, b_ctx, c_ctx_o, proj_c, bproj)
            xc = _outproj(xc, y_ctx.reshape(1, B * NC, D), wo, gate_c)
            m0, m1, m2 = rows(mod_c, 2)
            xc = _ffn(xc, ng[2:3], m0, m1, m2, *f2)
    return xl
```

```python
import functools
import math

import jax
import jax.numpy as jnp
from jax import lax
from jax.experimental import pallas as pl
from jax.experimental.pallas import tpu as pltpu

F32 = jnp.float32
BF16 = jnp.bfloat16

EPS = 1e-6
ROPE_THETA = 10000.0
GRID_W = 64
N_BRANCH = 3
N_MOD = 9
LRU_BLOCKS = 8
CONV_W = 4
LRU_C = 8.0
MLA_HEADS = 8
MLA_NOPE = 128
MLA_ROPE = 64
MLA_V = 128
MLA_QK = MLA_NOPE + MLA_ROPE
DIFF_HEADS = 8
DIFF_D = 64
DIFF_V = 2 * DIFF_D

LANE = 128
SUBLANE = 8
MLA_QK_PAD = 256
VMEM_LIMIT = 56 * 1024 * 1024
FFN_FK = 512
NORM_ROWS = 16
LOG2E = math.log2(math.e)
ATTN_TK = 1024
MLA_UNROLL = 16
DIFF_UNROLL = 8
MAX_SCORE_BOUND = 30.0


def _cparams(*sem):
    return pltpu.CompilerParams(dimension_semantics=sem, vmem_limit_bytes=VMEM_LIMIT)


def _tile(n, pref):
    if n <= pref:
        return n
    t = pref - pref % SUBLANE
    while t > SUBLANE and n % t:
        t -= SUBLANE
    assert n % t == 0, (n, pref)
    return t


def _sigmoid(x):
    return 0.5 * jnp.tanh(0.5 * x) + 0.5


def _norm_mod_into(h_scr, x_ref, g, m0, m1):
    tm = h_scr.shape[0]
    rows = NORM_ROWS if tm % NORM_ROWS == 0 else tm
    gm = g * (1.0 + m0)

    def body(c, carry):
        r0 = pl.multiple_of(c * rows, rows)
        x = x_ref[0, pl.ds(r0, rows), :]
        ms = jnp.mean(x * x, axis=-1, keepdims=True)
        h_scr[pl.ds(r0, rows), :] = ((x * lax.rsqrt(ms + EPS)) * gm + m1).astype(BF16)
        return carry

    lax.fori_loop(0, tm // rows, body, 0, unroll=math.gcd(tm // rows, 16))


def _mod_kernel(c_ref, w_ref, b_ref, o_ref):
    c = c_ref[...]
    sc = c * jax.nn.sigmoid(c)
    o_ref[0] = jnp.dot(sc, w_ref[0], preferred_element_type=F32) + b_ref[0]


def _compute_mod(cpad, w_mod, b_mod):
    L, D, ND = w_mod.shape
    R = cpad.shape[0]
    tn = _tile(ND, 1024)
    return pl.pallas_call(
        _mod_kernel,
        grid=(L, ND // tn),
        in_specs=[pl.BlockSpec((R, D), lambda l, j: (0, 0)),
                  pl.BlockSpec((1, D, tn), lambda l, j: (l, 0, j)),
                  pl.BlockSpec((1, 1, tn), lambda l, j: (l, 0, j))],
        out_specs=pl.BlockSpec((1, R, tn), lambda l, j: (l, 0, j)),
        out_shape=jax.ShapeDtypeStruct((L, R, ND), F32),
        compiler_params=_cparams("parallel", "parallel"),
        name="adaln_mod",
    )(cpad, w_mod, b_mod.reshape(L, 1, ND))


def _ffn_kernel(x_ref, g_ref, m0_ref, m1_ref, m2_ref, wg_ref, wu_ref, wd_ref, o_ref, h_scr, *, tail):
    f = pl.program_id(2)
    last = pl.num_programs(2) - 1
    fk = wg_ref.shape[1]

    @pl.when(f == 0)
    def _():
        _norm_mod_into(h_scr, x_ref, g_ref[...], m0_ref[0], m1_ref[0])
        o_ref[0] = jnp.zeros(o_ref.shape[1:], F32)

    def chunk(width):
        h = h_scr[...]
        g = jnp.dot(h, wg_ref[:, 0:width], preferred_element_type=F32)
        u = jnp.dot(h, wu_ref[:, 0:width], preferred_element_type=F32)
        a = (g * jax.nn.sigmoid(g)) * u
        o_ref[0] += jnp.dot(a.astype(BF16), wd_ref[0:width, :], preferred_element_type=F32)

    if tail == fk:
        chunk(fk)
    else:
        pl.when(f < last)(lambda: chunk(fk))
        pl.when(f == last)(lambda: chunk(tail))

    @pl.when(f == last)
    def _():
        o_ref[0] = x_ref[0] + (0.5 * m2_ref[0]) * o_ref[0]


def _ffn(x, g, m0, m1, m2, wg, wu, wd):
    B, N, D = x.shape
    F = wg.shape[1]
    tm = _tile(N, 512)
    fk = min(F, FFN_FK)
    nf = -(-F // fk)
    tail = F - (nf - 1) * fk
    tok = pl.BlockSpec((1, tm, D), lambda b, i, f: (b, i, 0))
    row = pl.BlockSpec((1, 1, D), lambda b, i, f: (b, 0, 0))
    return pl.pallas_call(
        functools.partial(_ffn_kernel, tail=tail),
        grid=(B, N // tm, nf),
        in_specs=[tok, pl.BlockSpec((1, D), lambda b, i, f: (0, 0)), row, row, row,
                  pl.BlockSpec((D, fk), lambda b, i, f: (0, f)),
                  pl.BlockSpec((D, fk), lambda b, i, f: (0, f)),
                  pl.BlockSpec((fk, D), lambda b, i, f: (f, 0))],
        out_specs=tok,
        out_shape=jax.ShapeDtypeStruct((B, N, D), F32),
        scratch_shapes=[pltpu.VMEM((tm, D), BF16)],
        compiler_params=_cparams("parallel", "parallel", "arbitrary"),
        name="swiglu_half_step",
    )(x, g, m0, m1, m2, wg, wu, wd)


def _inproj_kernel(x_ref, g_ref, m0_ref, m1_ref, w_ref, o_ref, h_scr):
    @pl.when(pl.program_id(2) == 0)
    def _():
        _norm_mod_into(h_scr, x_ref, g_ref[...], m0_ref[0], m1_ref[0])

    o_ref[0] = jnp.dot(h_scr[...], w_ref[...], preferred_element_type=F32)


def _inproj(x, g, m0, m1, w):
    B, N, D = x.shape
    NP = w.shape[1]
    tm = _tile(N, 1024)
    tn = _tile(NP, 1024)
    row = pl.BlockSpec((1, 1, D), lambda b, i, j: (b, 0, 0))
    return pl.pallas_call(
        _inproj_kernel,
        grid=(B, N // tm, NP // tn),
        in_specs=[pl.BlockSpec((1, tm, D), lambda b, i, j: (b, i, 0)),
                  pl.BlockSpec((1, D), lambda b, i, j: (0, 0)), row, row,
                  pl.BlockSpec((D, tn), lambda b, i, j: (0, j))],
        out_specs=pl.BlockSpec((1, tm, tn), lambda b, i, j: (b, i, j)),
        out_shape=jax.ShapeDtypeStruct((B, N, NP), F32),
        scratch_shapes=[pltpu.VMEM((tm, D), BF16)],
        compiler_params=_cparams("parallel", "parallel", "arbitrary"),
        name="mixer_in_proj",
    )(x, g, m0, m1, w)


def _lru_gates_and_scan(xc_of, wa_ref, ba_ref, wx_ref, bx_ref, lam_ref, h0_ref, store_h, hfin_ref,
                        a_scr, u_scr, carry_scr, *, reverse, tm, W):
    H = SUBLANE

    @pl.when(pl.program_id(1) == 0)
    def _():
        carry_scr[...] = h0_ref[0]

    sp = jax.nn.softplus(-lam_ref[...])
    lb = W // LRU_BLOCKS
    for n in range(LRU_BLOCKS):
        sl = slice(n * lb, (n + 1) * lb)
        xc = xc_of(sl)
        xb = xc.astype(BF16)
        r = _sigmoid(jnp.dot(xb, wa_ref[n], preferred_element_type=F32) + ba_ref[:, sl])
        ig = _sigmoid(jnp.dot(xb, wx_ref[n], preferred_element_type=F32) + bx_ref[:, sl])
        a = jnp.exp((-LRU_C) * r * sp[:, sl])
        a_scr[:, sl] = a
        u_scr[:, sl] = jnp.sqrt(1.0 - a * a) * (ig * xc)

    rowi = lax.broadcasted_iota(jnp.int32, (H, W), 0)
    ng = tm // H

    def group(gi, h):
        g = (ng - 1 - gi) if reverse else gi
        r0 = pl.multiple_of(g * H, H)
        a = a_scr[pl.ds(r0, H), :]
        u = u_scr[pl.ds(r0, H), :]
        for s in (1, 2, 4):
            sh = (H - s) if reverse else s
            valid = (rowi < H - s) if reverse else (rowi >= s)
            a_s = pltpu.roll(a, sh, 0)
            u_s = pltpu.roll(u, sh, 0)
            u = jnp.where(valid, a * u_s + u, u)
            a = jnp.where(valid, a * a_s, a)
        hh = a * h + u
        store_h(r0, hh)
        return hh[0:1, :] if reverse else hh[H - 1:H, :]

    h_last = lax.fori_loop(0, ng, group, carry_scr[...])
    carry_scr[...] = h_last
    hfin_ref[0] = h_last


def _lru_fwd_kernel(x_ref, xp_ref, xn_ref, cw_ref, cb_ref, wa_ref, ba_ref, wx_ref, bx_ref, lam_ref, h0_ref,
                    h_ref, xc_ref, hfin_ref, xs_scr, a_scr, u_scr, carry_scr, *, nt, tm, W):
    ti = pl.program_id(1)
    H = SUBLANE
    zero = jnp.zeros((H, W), F32)
    xs_scr[0:H, :] = jnp.where(ti > 0, xp_ref[0], zero)
    xs_scr[H:H + tm, :] = x_ref[0]
    xs_scr[H + tm:2 * H + tm, :] = jnp.where(ti < nt - 1, xn_ref[0], zero)
    left = CONV_W // 2

    def conv(sl):
        xc = cb_ref[:, sl] + xs_scr[H - left:H - left + tm, sl] * cw_ref[0:1, sl]
        for j in range(1, CONV_W):
            xc = xc + xs_scr[H - left + j:H - left + j + tm, sl] * cw_ref[j:j + 1, sl]
        xc_ref[0, :, sl] = xc
        return xc

    def store_h(r0, hh):
        h_ref[0, pl.ds(r0, H), :] = hh

    _lru_gates_and_scan(conv, wa_ref, ba_ref, wx_ref, bx_ref, lam_ref, h0_ref, store_h, hfin_ref,
                        a_scr, u_scr, carry_scr, reverse=False, tm=tm, W=W)


def _lru_rev_kernel(xc_ref, hf_ref, rg_ref, wa_ref, ba_ref, wx_ref, bx_ref, lam_ref, h0_ref,
                    y_ref, hfin_ref, hb_scr, a_scr, u_scr, carry_scr, *, tm, W):
    def store_h(r0, hh):
        hb_scr[pl.ds(r0, SUBLANE), :] = hh

    _lru_gates_and_scan(lambda sl: xc_ref[0, :, sl], wa_ref, ba_ref, wx_ref, bx_ref, lam_ref, h0_ref,
                        store_h, hfin_ref, a_scr, u_scr, carry_scr, reverse=True, tm=tm, W=W)
    lb = W // LRU_BLOCKS
    for n in range(LRU_BLOCKS):
        sl = slice(n * lb, (n + 1) * lb)
        y_ref[0, :, sl] = ((hf_ref[0, :, sl] + hb_scr[:, sl]) * jax.nn.gelu(rg_ref[0, :, sl])).astype(BF16)


def _lru_specs(W):
    vec = pl.BlockSpec((1, W), lambda b, i: (0, 0))
    blk = pl.BlockSpec((LRU_BLOCKS, W // LRU_BLOCKS, W // LRU_BLOCKS), lambda b, i: (0, 0, 0))
    state = pl.BlockSpec((1, 1, W), lambda b, i: (b, 0, 0))
    return vec, blk, state


def _lru_fwd(proj, col_blk, h0, cw, cb, wa, ba, wx, bx, lam, *, W):
    B, N, _ = proj.shape
    tm = _tile(N, 256)
    nt = N // tm
    r8 = tm // SUBLANE
    nb8 = N // SUBLANE
    vec, blk, state = _lru_specs(W)
    tok = pl.BlockSpec((1, tm, W), lambda b, i: (b, i, 0))
    full = jax.ShapeDtypeStruct((B, N, W), F32)
    return pl.pallas_call(
        functools.partial(_lru_fwd_kernel, nt=nt, tm=tm, W=W),
        grid=(B, nt),
        in_specs=[pl.BlockSpec((1, tm, W), lambda b, i: (b, i, col_blk)),
                  pl.BlockSpec((1, SUBLANE, W), lambda b, i: (b, jnp.maximum(i * r8 - 1, 0), col_blk)),
                  pl.BlockSpec((1, SUBLANE, W), lambda b, i: (b, jnp.minimum((i + 1) * r8, nb8 - 1), col_blk)),
                  pl.BlockSpec((CONV_W, W), lambda b, i: (0, 0)), vec, blk, vec, blk, vec, vec, state],
        out_specs=[tok, tok, state],
        out_shape=[full, full, jax.ShapeDtypeStruct((B, 1, W), F32)],
        scratch_shapes=[pltpu.VMEM((tm + 2 * SUBLANE, W), F32), pltpu.VMEM((tm, W), F32),
                        pltpu.VMEM((tm, W), F32), pltpu.VMEM((1, W), F32)],
        compiler_params=_cparams("parallel", "arbitrary"),
        name="rglru_fwd",
    )(proj, proj, proj, cw, cb, wa, ba, wx, bx, lam, h0)


def _lru_rev(xc, hf, proj, gate_blk, h0, wa, ba, wx, bx, lam, *, W):
    B, N, _ = xc.shape
    tm = _tile(N, 256)
    nt = N // tm
    vec, blk, state = _lru_specs(W)
    tok = pl.BlockSpec((1, tm, W), lambda b, i: (b, nt - 1 - i, 0))
    return pl.pallas_call(
        functools.partial(_lru_rev_kernel, tm=tm, W=W),
        grid=(B, nt),
        in_specs=[tok, tok, pl.BlockSpec((1, tm, W), lambda b, i: (b, nt - 1 - i, gate_blk)),
                  blk, vec, blk, vec, vec, state],
        out_specs=[tok, state],
        out_shape=[jax.ShapeDtypeStruct((B, N, W), BF16), jax.ShapeDtypeStruct((B, 1, W), F32)],
        scratch_shapes=[pltpu.VMEM((tm, W), F32), pltpu.VMEM((tm, W), F32),
                        pltpu.VMEM((tm, W), F32), pltpu.VMEM((1, W), F32)],
        compiler_params=_cparams("parallel", "arbitrary"),
        name="rglru_rev_gate",
    )(xc, hf, proj, wa, ba, wx, bx, lam, h0)


def _rope(x, cosf, sins, perm):
    return x * cosf + jnp.dot(x.astype(BF16), perm, preferred_element_type=F32) * sins


def _mla_prep_kernel(qc_ref, kvc_ref, kr_ref, gqc_ref, gkvc_ref, wuq_ref, wukv_ref, gq_ref, gk_ref,
                     cos_ref, sin_ref, perm_ref, q_ref, k_ref, v_ref):
    cosf, sins, perm = cos_ref[...], sin_ref[...], perm_ref[...]
    gqs = gq_ref[...] * (MLA_QK ** -0.5 * LOG2E)

    qc = qc_ref[0]
    qn = (qc * lax.rsqrt(jnp.mean(qc * qc, axis=-1, keepdims=True) + EPS)) * gqc_ref[...]
    qf = jnp.dot(qn.astype(BF16), wuq_ref[...], preferred_element_type=F32)
    kvc = kvc_ref[0]
    kvn = (kvc * lax.rsqrt(jnp.mean(kvc * kvc, axis=-1, keepdims=True) + EPS)) * gkvc_ref[...]
    kvf = jnp.dot(kvn.astype(BF16), wukv_ref[...], preferred_element_type=F32)

    kr = kr_ref[0]
    ss_kr = jnp.sum(kr * kr, axis=-1, keepdims=True)
    kr_rot = _rope(kr * gk_ref[:, LANE:], cosf, sins, perm)

    for h in range(MLA_HEADS):
        o = h * MLA_QK_PAD
        qa = qf[:, o:o + LANE]
        qb = qf[:, o + LANE:o + 2 * LANE]
        ss = jnp.sum(qa * qa + qb * qb, axis=-1, keepdims=True)
        rinv = lax.rsqrt(ss * (1.0 / MLA_QK) + EPS)
        q_ref[0, h, :, 0:LANE] = ((qa * rinv) * gqs[:, 0:LANE]).astype(BF16)
        q_ref[0, h, :, LANE:] = _rope((qb * rinv) * gqs[:, LANE:], cosf, sins, perm).astype(BF16)

        kn = kvf[:, o:o + LANE]
        ssk = jnp.sum(kn * kn, axis=-1, keepdims=True) + ss_kr
        rk = lax.rsqrt(ssk * (1.0 / MLA_QK) + EPS)
        k_ref[0, h, :, 0:LANE] = ((kn * rk) * gk_ref[:, 0:LANE]).astype(BF16)
        k_ref[0, h, :, LANE:] = (kr_rot * rk).astype(BF16)
        v_ref[0, h] = kvf[:, o + LANE:o + 2 * LANE].astype(BF16)


def _mla_prep(proj, cols, gqc, gkvc, wuq, wukv, gq, gk, cosf, sins, perm):
    B, N, _ = proj.shape
    tm = _tile(N, 256)
    QR, KR = gqc.shape[1], gkvc.shape[1]
    HP = MLA_HEADS * MLA_QK_PAD
    tab = pl.BlockSpec((tm, LANE), lambda b, i: (i, 0))

    def full(a):
        return pl.BlockSpec(a.shape, lambda b, i: (0,) * a.ndim)

    hd = lambda w: pl.BlockSpec((1, MLA_HEADS, tm, w), lambda b, i: (b, 0, i, 0))
    return pl.pallas_call(
        _mla_prep_kernel,
        grid=(B, N // tm),
        in_specs=[pl.BlockSpec((1, tm, QR), lambda b, i: (b, i, cols["qc"] // QR)),
                  pl.BlockSpec((1, tm, KR), lambda b, i: (b, i, cols["kvc"] // KR)),
                  pl.BlockSpec((1, tm, LANE), lambda b, i: (b, i, cols["kr"] // LANE)),
                  full(gqc), full(gkvc), full(wuq), full(wukv), full(gq), full(gk), tab, tab, full(perm)],
        out_specs=[hd(MLA_QK_PAD), hd(MLA_QK_PAD), hd(MLA_V)],
        out_shape=[jax.ShapeDtypeStruct((B, MLA_HEADS, N, MLA_QK_PAD), BF16),
                   jax.ShapeDtypeStruct((B, MLA_HEADS, N, MLA_QK_PAD), BF16),
                   jax.ShapeDtypeStruct((B, MLA_HEADS, N, MLA_V), BF16)],
        compiler_params=_cparams("parallel", "parallel"),
        name="mla_qkv_prep",
    )(proj, proj, proj, gqc, gkvc, wuq, wukv, gq, gk, cosf, sins, perm)


def _diff_prep_kernel(dq_ref, dk_ref, dv_ref, gq_ref, gk_ref, cos_ref, sin_ref, perm_ref, seg_ref,
                      q_ref, k_ref, v_ref):
    cosf, sins, perm, seg = cos_ref[...], sin_ref[...], perm_ref[...], seg_ref[...]
    gqs = gq_ref[...] * (DIFF_D ** -0.5 * LOG2E)
    gk = gk_ref[...]

    def norm_rope(x, g):
        ss = jnp.dot((x * x).astype(BF16), seg, preferred_element_type=F32)
        return _rope((x * lax.rsqrt(ss * (1.0 / DIFF_D) + EPS)) * g, cosf, sins, perm)

    for h in range(DIFF_HEADS):
        sl = slice(h * LANE, (h + 1) * LANE)
        q_ref[0, h] = norm_rope(dq_ref[0, :, sl], gqs).astype(BF16)
        k_ref[0, h] = norm_rope(dk_ref[0, :, sl], gk).astype(BF16)
        v_ref[0, h] = dv_ref[0, :, sl].astype(BF16)


def _diff_prep(proj, cols, gq2, gk2, cosf, sins, perm, seg):
    B, N, _ = proj.shape
    tm = _tile(N, 256)
    HW = DIFF_HEADS * LANE
    tab = pl.BlockSpec((tm, LANE), lambda b, i: (i, 0))
    mat = pl.BlockSpec((LANE, LANE), lambda b, i: (0, 0))
    g = pl.BlockSpec((1, LANE), lambda b, i: (0, 0))
    hd = pl.BlockSpec((1, DIFF_HEADS, tm, LANE), lambda b, i: (b, 0, i, 0))
    shp = jax.ShapeDtypeStruct((B, DIFF_HEADS, N, LANE), BF16)
    return pl.pallas_call(
        _diff_prep_kernel,
        grid=(B, N // tm),
        in_specs=[pl.BlockSpec((1, tm, HW), lambda b, i: (b, i, cols["dq"] // HW)),
                  pl.BlockSpec((1, tm, HW), lambda b, i: (b, i, cols["dk"] // HW)),
                  pl.BlockSpec((1, tm, HW), lambda b, i: (b, i, cols["dv"] // HW)),
                  g, g, tab, tab, mat, mat],
        out_specs=[hd, hd, hd],
        out_shape=[shp, shp, shp],
        compiler_params=_cparams("parallel", "parallel"),
        name="diff_qkv_prep",
    )(proj, proj, proj, gq2, gk2, cosf, sins, perm, seg)


def _softmax_step(s, v, shift, m_scr, l_scr, acc_scr):
    tk = s.shape[1]
    if shift is not None:
        p = jnp.exp2(s - shift)
        part = p[:, 0:LANE]
        for c in range(1, tk // LANE):
            part = part + p[:, c * LANE:(c + 1) * LANE]
        l_scr[...] += part
        acc_scr[...] += jnp.dot(p.astype(BF16), v, preferred_element_type=F32)
        return
    m_prev = m_scr[...]
    m_new = jnp.maximum(m_prev, jnp.max(s, axis=1, keepdims=True))
    alpha = jnp.exp2(m_prev - m_new)
    m_b = jnp.concatenate([m_new] * (tk // LANE), axis=1) if tk % LANE == 0 else m_new[:, 0:1]
    p = jnp.exp2(s - m_b)
    l_scr[...] = alpha * l_scr[...] + jnp.sum(p, axis=1, keepdims=True)
    acc_scr[...] = alpha * acc_scr[...] + jnp.dot(p.astype(BF16), v, preferred_element_type=F32)
    m_scr[...] = m_new


def _row_sum(l_scr, bounded):
    return jnp.sum(l_scr[...], axis=1, keepdims=True) if bounded else l_scr[...]


def _qk(q, k):
    return lax.dot_general(q, k, (((1,), (1,)), ((), ())), preferred_element_type=F32)


def _kv_loop(kv_refs, tk, step, unroll):
    for c in range(len(kv_refs) // 2):
        k_ref, v_ref = kv_refs[2 * c], kv_refs[2 * c + 1]
        nk = k_ref.shape[2]
        t = _tile(nk, tk)

        def body(j, carry, k_ref=k_ref, v_ref=v_ref, t=t):
            r0 = pl.multiple_of(j * t, t)
            step(k_ref[0, 0, pl.ds(r0, t), :], v_ref[0, 0, pl.ds(r0, t), :])
            return carry

        lax.fori_loop(0, nk // t, body, 0, unroll=math.gcd(nk // t, unroll))


def _mla_attn_kernel(*refs, n_kv, tk, bounded):
    shift_ref, q_ref = refs[0], refs[1]
    kv_refs = refs[2:2 + 2 * n_kv]
    o_ref = refs[2 + 2 * n_kv]
    m_scr, l_scr, acc_scr = refs[3 + 2 * n_kv:]
    if not bounded:
        m_scr[...] = jnp.full(m_scr.shape, -jnp.inf, F32)
    l_scr[...] = jnp.zeros(l_scr.shape, F32)
    acc_scr[...] = jnp.zeros(acc_scr.shape, F32)
    q = q_ref[0, 0]
    shift = shift_ref[0] if bounded else None
    _kv_loop(kv_refs, tk, lambda k, v: _softmax_step(_qk(q, k), v, shift, m_scr, l_scr, acc_scr),
             MLA_UNROLL if bounded else 1)
    o_ref[0] = (acc_scr[...] / _row_sum(l_scr, bounded)).astype(BF16)


def _mla_attn(q, kvs, shift, bounded):
    B, H, N, DQ = q.shape
    tq = _tile(N, 1024)
    in_specs = [pl.BlockSpec(memory_space=pltpu.SMEM),
                pl.BlockSpec((1, 1, tq, DQ), lambda b, h, i: (b, h, i, 0))]
    args = [shift, q]
    for k, v in kvs:
        nk = k.shape[2]
        in_specs += [pl.BlockSpec((1, 1, nk, DQ), lambda b, h, i: (b, h, 0, 0)),
                     pl.BlockSpec((1, 1, nk, MLA_V), lambda b, h, i: (b, h, 0, 0))]
        args += [k, v]
    stat = pltpu.VMEM((tq, LANE), F32)
    return pl.pallas_call(
        functools.partial(_mla_attn_kernel, n_kv=len(kvs), tk=ATTN_TK, bounded=bounded),
        grid=(B, H, N // tq),
        in_specs=in_specs,
        out_specs=pl.BlockSpec((1, tq, MLA_V), lambda b, h, i: (b, i, h)),
        out_shape=jax.ShapeDtypeStruct((B, N, H * MLA_V), BF16),
        scratch_shapes=[stat, stat, stat],
        compiler_params=_cparams("parallel", "parallel", "arbitrary"),
        name="mla_attention",
    )(*args)


def _diff_attn_kernel(*refs, n_kv, tk, lam_init, bounded):
    shift_ref, q_ref = refs[0], refs[1]
    kv_refs = refs[2:2 + 2 * n_kv]
    lamp_ref, gout_ref, o_ref = refs[2 + 2 * n_kv:5 + 2 * n_kv]
    m0, l0, acc0, m1, l1, acc1 = refs[5 + 2 * n_kv:]
    if not bounded:
        for m in (m0, m1):
            m[...] = jnp.full(m.shape, -jnp.inf, F32)
    for z in (l0, l1, acc0, acc1):
        z[...] = jnp.zeros(z.shape, F32)
    q = q_ref[0, 0]
    first = lax.broadcasted_iota(jnp.int32, q.shape, 1) < DIFF_D
    zero = jnp.zeros(q.shape, BF16)
    qa = jnp.where(first, q, zero)
    qb = jnp.where(first, zero, q)
    shift = shift_ref[0] if bounded else None

    def step(k, v):
        _softmax_step(_qk(qa, k), v, shift, m0, l0, acc0)
        _softmax_step(_qk(qb, k), v, shift, m1, l1, acc1)

    _kv_loop(kv_refs, tk, step, DIFF_UNROLL if bounded else 1)

    lp = lamp_ref[...]
    lam = (jnp.exp(jnp.sum(lp[0:1] * lp[1:2], keepdims=True))
           - jnp.exp(jnp.sum(lp[2:3] * lp[3:4], keepdims=True)) + lam_init)
    o = acc0[...] / _row_sum(l0, bounded) - lam * (acc1[...] / _row_sum(l1, bounded))
    on = (o * lax.rsqrt(jnp.mean(o * o, axis=-1, keepdims=True) + EPS)) * gout_ref[...]
    o_ref[0] = (on * (1.0 - lam_init)).astype(BF16)


def _diff_attn(q, kvs, lam_p, g_out, lam_init, shift, bounded):
    B, H, N, DQ = q.shape
    tq = _tile(N, 1024)
    in_specs = [pl.BlockSpec(memory_space=pltpu.SMEM),
                pl.BlockSpec((1, 1, tq, DQ), lambda b, h, i: (b, h, i, 0))]
    args = [shift, q]
    for k, v in kvs:
        nk = k.shape[2]
        kv_spec = pl.BlockSpec((1, 1, nk, LANE), lambda b, h, i: (b, h, 0, 0))
        in_specs += [kv_spec, kv_spec]
        args += [k, v]
    in_specs += [pl.BlockSpec(lam_p.shape, lambda b, h, i: (0, 0)),
                 pl.BlockSpec((1, DIFF_V), lambda b, h, i: (0, 0))]
    args += [lam_p, g_out]
    stat = pltpu.VMEM((tq, LANE), F32)
    return pl.pallas_call(
        functools.partial(_diff_attn_kernel, n_kv=len(kvs), tk=ATTN_TK, lam_init=lam_init, bounded=bounded),
        grid=(B, H, N // tq),
        in_specs=in_specs,
        out_specs=pl.BlockSpec((1, tq, DIFF_V), lambda b, h, i: (b, i, h)),
        out_shape=jax.ShapeDtypeStruct((B, N, H * DIFF_V), BF16),
        scratch_shapes=[stat] * 6,
        compiler_params=_cparams("parallel", "parallel", "arbitrary"),
        name="diff_attention",
    )(*args)


def _merge_kernel(a_ref, b_ref, c_ref, gt_ref, p_ref, o_ref):
    D = o_ref.shape[2]
    acc = None
    for j, br_ref in enumerate((a_ref, b_ref, c_ref)):
        y = _sigmoid(gt_ref[0, :, j * D:(j + 1) * D]) * jnp.dot(br_ref[0], p_ref[j], preferred_element_type=F32)
        acc = y if acc is None else acc + y
    o_ref[0] = acc.astype(BF16)


def _merge(a, b, c, proj, bproj):
    B, N, W = a.shape
    D = bproj.shape[2]
    tm = _tile(N, 256)
    br = pl.BlockSpec((1, tm, W), lambda b_, i: (b_, i, 0))
    return pl.pallas_call(
        _merge_kernel,
        grid=(B, N // tm),
        in_specs=[br, br, br,
                  pl.BlockSpec((1, tm, N_BRANCH * D), lambda b_, i: (b_, i, 0)),
                  pl.BlockSpec((N_BRANCH, W, D), lambda b_, i: (0, 0, 0))],
        out_specs=pl.BlockSpec((1, tm, D), lambda b_, i: (b_, i, 0)),
        out_shape=jax.ShapeDtypeStruct((B, N, D), BF16),
        compiler_params=_cparams("parallel", "parallel"),
        name="branch_merge",
    )(a, b, c, proj, bproj)


def _outproj_kernel(x_ref, y_ref, w_ref, m2_ref, o_ref):
    o_ref[0] = x_ref[0] + m2_ref[0] * jnp.dot(y_ref[0], w_ref[...], preferred_element_type=F32)


def _outproj(x, y, w, m2):
    B, N, D = x.shape
    tm = _tile(N, 512)
    tok = pl.BlockSpec((1, tm, D), lambda b, i: (b, i, 0))
    return pl.pallas_call(
        _outproj_kernel,
        grid=(B, N // tm),
        in_specs=[tok, tok, pl.BlockSpec((D, D), lambda b, i: (0, 0)),
                  pl.BlockSpec((1, 1, D), lambda b, i: (b, 0, 0))],
        out_specs=tok,
        out_shape=jax.ShapeDtypeStruct((B, N, D), F32),
        compiler_params=_cparams("parallel", "parallel"),
        name="mixer_out_proj",
    )(x, y, w, m2)


def _rope_tables(n_tokens, identity):
    half = MLA_ROPE // 2
    if identity:
        return jnp.ones((n_tokens, LANE), F32), jnp.zeros((n_tokens, LANE), F32)
    rows = n_tokens // GRID_W
    axis_dim = MLA_ROPE // 2
    inv_freq = ROPE_THETA ** (-jnp.arange(0, axis_dim, 2, dtype=F32) / axis_dim)
    nf = axis_dim // 2
    row_ang = jnp.arange(rows, dtype=F32)[:, None] * inv_freq
    col_ang = jnp.arange(GRID_W, dtype=F32)[:, None] * inv_freq
    ang = jnp.concatenate([jnp.broadcast_to(row_ang[:, None, :], (rows, GRID_W, nf)),
                           jnp.broadcast_to(col_ang[None, :, :], (rows, GRID_W, nf))], axis=-1)
    ang = ang.reshape(rows * GRID_W, axis_dim)
    cos, sin = jnp.cos(ang), jnp.sin(ang)
    assert half == axis_dim
    return jnp.concatenate([cos, cos, cos, cos], axis=1), jnp.concatenate([-sin, sin, -sin, sin], axis=1)


def _lane_matrices():
    i = jnp.arange(LANE)
    half = MLA_ROPE // 2
    partner = jnp.where(i % MLA_ROPE < half, i + half, i - half)
    perm = (i[:, None] == partner[None, :]).astype(BF16)
    seg = (i[:, None] // DIFF_D == i[None, :] // DIFF_D).astype(BF16)
    return perm, seg


def _in_proj_layout(D, W):
    order = [("gt", N_BRANCH * D), ("rx", W), ("rg", W), ("dq", DIFF_HEADS * 2 * DIFF_D),
             ("dk", DIFF_HEADS * 2 * DIFF_D), ("dv", DIFF_HEADS * DIFF_V)]
    cols, off = {}, 0
    for name, width in order:
        if name != "gt":
            off = -(-off // width) * width
        cols[name] = off
        off += width
    return cols, off


def kernel(x, c, ctx, c_ctx, w_mod, b_mod, norm_g, ffn_w_gate, ffn_w_up, ffn_w_down, w_in, lru_conv_w,
           lru_conv_b, lru_wa, lru_ba, lru_wx, lru_bx, lru_lambda, mla_qc_norm, mla_kvc_norm, mla_w_uq,
           mla_w_ukv, mla_q_norm, mla_k_norm, diff_q_norm, diff_k_norm, diff_lambda, diff_out_norm,
           branch_proj, w_out):
    B, N, D = x.shape
    NC = ctx.shape[1]
    L = w_mod.shape[0]
    W = lru_conv_w.shape[-1]
    QR, KR = mla_qc_norm.shape[-1], mla_kvc_norm.shape[-1]
    dqw = DIFF_HEADS * 2 * DIFF_D

    sizes = dict(rx=W, rg=W, qc=QR, kvc=KR, kr=MLA_ROPE, dq=dqw, dk=dqw, dv=DIFF_HEADS * DIFF_V,
                 gt=N_BRANCH * D)
    src, off = {}, 0
    for name in ("rx", "rg", "qc", "kvc", "kr", "dq", "dk", "dv", "gt"):
        src[name] = off
        off += sizes[name]
    cols, off = _in_proj_layout(D, W)
    for name, width in (("qc", QR), ("kvc", KR), ("kr", LANE)):
        off = -(-off // width) * width
        cols[name] = off
        off += width
    NP = -(-off // 2048) * 2048 if off > 2048 else off

    def pad_w_in(w):
        out = jnp.zeros((D, NP), BF16)
        for name in sizes:
            out = lax.dynamic_update_slice(out, w[:, src[name]:src[name] + sizes[name]].astype(BF16),
                                           (0, cols[name]))
        return out

    def pad_heads(w, width):
        K = w.shape[0]
        w3 = w.reshape(K, MLA_HEADS, width)
        return jnp.pad(w3, ((0, 0), (0, 0), (0, MLA_QK_PAD - width))).reshape(K, MLA_HEADS * MLA_QK_PAD)

    R = -(-(B + 1) // SUBLANE) * SUBLANE
    cpad = jnp.zeros((R, D), F32).at[:B].set(c).at[B].set(c_ctx)
    mod = _compute_mod(cpad, w_mod, b_mod).reshape(L, R, N_MOD, D)

    perm, seg = _lane_matrices()
    tabs_l = _rope_tables(N, identity=False) + (perm,)
    tabs_c = _rope_tables(NC, identity=True) + (perm,)

    xl, xc = x, ctx.reshape(1, B * NC, D)
    for l in range(L):
        last = l == L - 1
        lam_init = 0.8 - 0.6 * math.exp(-0.3 * l)
        mod_l = mod[l, :B]
        mod_c = mod[l, B:B + 1]

        def rows(m, s):
            return m[:, 3 * s:3 * s + 1], m[:, 3 * s + 1:3 * s + 2], m[:, 3 * s + 2:3 * s + 3]

        ng = norm_g[l]
        f1 = (ffn_w_gate[l, 0].astype(BF16), ffn_w_up[l, 0].astype(BF16), ffn_w_down[l, 0].astype(BF16))
        f2 = (ffn_w_gate[l, 1].astype(BF16), ffn_w_up[l, 1].astype(BF16), ffn_w_down[l, 1].astype(BF16))
        w_in_p = pad_w_in(w_in[l])
        wuq = pad_heads(mla_w_uq[l], MLA_QK).astype(BF16)
        wukv = mla_w_ukv[l].astype(BF16)
        gq = jnp.pad(mla_q_norm[l], (0, MLA_QK_PAD - MLA_QK)).reshape(1, MLA_QK_PAD)
        gk = jnp.pad(mla_k_norm[l], (0, MLA_QK_PAD - MLA_QK)).reshape(1, MLA_QK_PAD)
        gqc = mla_qc_norm[l].reshape(1, QR)
        gkvc = mla_kvc_norm[l].reshape(1, KR)
        gdq = jnp.tile(diff_q_norm[l], 2).reshape(1, LANE)
        gdk = jnp.tile(diff_k_norm[l], 2).reshape(1, LANE)
        gdo = diff_out_norm[l].reshape(1, DIFF_V)
        bproj = branch_proj[l].astype(BF16)
        wo = w_out[l].astype(BF16)
        lru = dict(cw=lru_conv_w[l], cb=lru_conv_b[l].reshape(1, W))

        bound_m = math.sqrt(MLA_QK) * jnp.max(jnp.abs(mla_q_norm[l])) * jnp.max(jnp.abs(mla_k_norm[l]))
        bound_d = math.sqrt(DIFF_D) * jnp.max(jnp.abs(diff_q_norm[l])) * jnp.max(jnp.abs(diff_k_norm[l]))

        def mla_attend(q, kvs):
            shift = (bound_m * LOG2E).reshape(1)
            return lax.cond(bound_m <= MAX_SCORE_BOUND,
                            lambda: _mla_attn(q, kvs, shift, True), lambda: _mla_attn(q, kvs, shift, False))

        def diff_attend(q, kvs):
            shift = (bound_d * LOG2E).reshape(1)
            args = (diff_lambda[l], gdo, lam_init, shift)
            return lax.cond(bound_d <= MAX_SCORE_BOUND,
                            lambda: _diff_attn(q, kvs, *args, True), lambda: _diff_attn(q, kvs, *args, False))

        m0, m1, m2 = rows(mod_l, 0)
        xl = _ffn(xl, ng[0:1], m0, m1, m2, *f1)
        m0, m1, m2 = rows(mod_c, 0)
        xc = _ffn(xc, ng[0:1], m0, m1, m2, *f1)

        m0, m1, gate_l = rows(mod_l, 1)
        proj_l = _inproj(xl, ng[1:2], m0, m1, w_in_p)
        m0, m1, gate_c = rows(mod_c, 1)
        proj_c = _inproj(xc, ng[1:2], m0, m1, w_in_p).reshape(B, NC, NP)

        h_zero = jnp.zeros((B, 1, W), F32)
        rx, rg = cols["rx"] // W, cols["rg"] // W
        prm = [dict(wa=lru_wa[l, d].astype(BF16), ba=lru_ba[l, d].reshape(1, W), wx=lru_wx[l, d].astype(BF16),
                    bx=lru_bx[l, d].reshape(1, W), lam=lru_lambda[l, d].reshape(1, W)) for d in range(2)]
        hf_c, xc_c, fin_c = _lru_fwd(proj_c, rx, h_zero, lru["cw"], lru["cb"], W=W, **prm[0])
        hf_l, xc_l, _ = _lru_fwd(proj_l, rx, fin_c, lru["cw"], lru["cb"], W=W, **prm[0])
        a_ctx, fin_c = _lru_rev(xc_c, hf_c, proj_c, rg, h_zero, W=W, **prm[1])
        a_lat, _ = _lru_rev(xc_l, hf_l, proj_l, rg, fin_c, W=W, **prm[1])

        q_ml, k_ml, v_ml = _mla_prep(proj_l, cols, gqc, gkvc, wuq, wukv, gq, gk, *tabs_l)
        q_mc, k_mc, v_mc = _mla_prep(proj_c, cols, gqc, gkvc, wuq, wukv, gq, gk, *tabs_c)
        b_lat = mla_attend(q_ml, [(k_ml, v_ml), (k_mc, v_mc)])

        q_dl, k_dl, v_dl = _diff_prep(proj_l, cols, gdq, gdk, *tabs_l, seg)
        q_dc, k_dc, v_dc = _diff_prep(proj_c, cols, gdq, gdk, *tabs_c, seg)
        c_lat = diff_attend(q_dl, [(k_dl, v_dl), (k_dc, v_dc)])

        y_lat = _merge(a_lat, b_lat, c_lat, proj_l, bproj)
        xl = _outproj(xl, y_lat, wo, gate_l)

        m0, m1, m2 = rows(mod_l, 2)
        xl = _ffn(xl, ng[2:3], m0, m1, m2, *f2)

        if not last:
            b_ctx = mla_attend(q_mc, [(k_mc, v_mc)])
            c_ctx_o = diff_attend(q_dc, [(k_dc, v_dc)])
            y_ctx = _merge(a_ctx, b_ctx, c_ctx_o, proj_c, bproj)
            xc = _outproj(xc, y_ctx.reshape(1, B * NC, D), wo, gate_c)
            m0, m1, m2 = rows(mod_c, 2)
            xc = _ffn(xc, ng[2:3], m0, m1, m2, *f2)
    return xl
```
